```python
import math
import jax
import jax.numpy as jnp
from jax import lax
import numpy as np

D_MODEL = 1024
BATCH = 32
SEQ = 2048
DEPTH = 2

CTX_LEN = 256
GRID_W = 64
RMS_EPS = 1e-6
L2_EPS = 1e-6
A_WIDTH = D_MODEL // 2
GDN_HEADS = 8
GDN_DK = (D_MODEL // 2) // GDN_HEADS
GDN_DV = (D_MODEL // 2) // GDN_HEADS
GDN_CHUNK = 64
QKV_WIDTH = 2 * GDN_HEADS * GDN_DK + GDN_HEADS * GDN_DV
EVEN_SPLITS = (A_WIDTH, A_WIDTH, A_WIDTH, QKV_WIDTH, GDN_HEADS * GDN_DV, GDN_HEADS, GDN_HEADS, GDN_HEADS, GDN_HEADS)
EVEN_IN = sum(EVEN_SPLITS)
EVEN_OUT = A_WIDTH + GDN_HEADS * GDN_DV
ATTN_HEADS = 8
ATTN_KV_HEADS = 2
ATTN_GROUP = ATTN_HEADS // ATTN_KV_HEADS
ATTN_HEAD_DIM = D_MODEL // ATTN_HEADS
ODD_IN = (ATTN_HEADS + 2 * ATTN_KV_HEADS) * ATTN_HEAD_DIM
Q_BLOCK = 128
ROPE_THETA = 10000.0
N_GROUPS = 8
EXPERTS_PER_GROUP = 8
N_EXPERTS = N_GROUPS * EXPERTS_PER_GROUP
TOP_K = 2
D_EXPERT = (3 * D_MODEL) // 8
MOE_BLOCK = 256

kernel_name = 'hybrid_conv_deltanet_gqa_hier_moe_dit'

F32 = jnp.float32


def _split(t, sizes):
    idx = [int(s) for s in np.cumsum(sizes)[:-1]]
    return jnp.split(t, idx, axis=-1)


def _rmsnorm(x, g):
    xf = x.astype(F32)
    return (xf * lax.rsqrt(jnp.mean(xf * xf, axis=-1, keepdims=True) + RMS_EPS)).astype(x.dtype) * g


def _l2norm(x):
    xf = x.astype(F32)
    return (xf * lax.rsqrt(jnp.sum(xf * xf, axis=-1, keepdims=True) + L2_EPS)).astype(x.dtype)


def _dwconv3(x, w):
    ch = x.shape[-1]
    return lax.conv_general_dilated(x, w[:, None, :].astype(x.dtype), window_strides=(1,), padding=((1, 1),),
                                    dimension_numbers=('NWC', 'WIO', 'NWC'), feature_group_count=ch)


def _axial_rope(rows, cols):
    n_freq = ATTN_HEAD_DIM // 4
    inv = ROPE_THETA ** (-jnp.arange(n_freq, dtype=F32) / n_freq)
    ang = jnp.concatenate([rows.astype(F32)[:, None] * inv, cols.astype(F32)[:, None] * inv], axis=-1)
    return jnp.cos(ang), jnp.sin(ang)


def _rope(x, cos, sin):
    xf = x.astype(F32).reshape(*x.shape[:-1], -1, 2)
    x1, x2 = xf[..., 0], xf[..., 1]
    out = jnp.stack([x1 * cos - x2 * sin, x1 * sin + x2 * cos], axis=-1)
    return out.reshape(x.shape).astype(x.dtype)


def _gdn_chunked(q, k, v, g, beta, s0):
    b, h, l, dk = q.shape
    dv = v.shape[-1]
    c = GDN_CHUNK
    n = l // c
    q, k, v = (t.astype(F32).reshape(b, h, n, c, -1) for t in (q, k, v))
    g = g.astype(F32).reshape(b, h, n, c)
    beta = beta.astype(F32).reshape(b, h, n, c)
    gc = jnp.cumsum(g, axis=-1)
    incl = jnp.tril(jnp.ones((c, c), dtype=bool))
    strict = jnp.tril(jnp.ones((c, c), dtype=bool), -1)
    decay = jnp.exp(jnp.where(incl, gc[..., :, None] - gc[..., None, :], -jnp.inf))
    kb = k * beta[..., None]
    a_mat = jnp.eye(c, dtype=F32) + jnp.where(strict, jnp.einsum('bhnid,bhnjd->bhnij', kb, k) * decay, 0.0)
    rhs = jnp.concatenate([v * beta[..., None], kb * jnp.exp(gc)[..., None]], axis=-1)
    sol = lax.linalg.triangular_solve(a_mat, rhs, left_side=True, lower=True, unit_diagonal=True)
    u, w = sol[..., :dv], sol[..., dv:]
    attn = jnp.einsum('bhnid,bhnjd->bhnij', q, k) * decay
    qg = q * jnp.exp(gc)[..., None]
    kg = k * jnp.exp(gc[..., -1:] - gc)[..., None]
    g_last = jnp.exp(gc[..., -1])
    xs = tuple(jnp.moveaxis(t, 2, 0) for t in (u, w, attn, qg, kg, g_last))

    def step(s, inp):
        u_n, w_n, a_n, qg_n, kg_n, gl_n = inp
        v_new = u_n - jnp.einsum('bhcd,bhde->bhce', w_n, s)
        o_n = jnp.einsum('bhcd,bhde->bhce', qg_n, s) + jnp.einsum('bhij,bhje->bhie', a_n, v_new)
        s = s * gl_n[..., None, None] + jnp.einsum('bhcd,bhce->bhde', kg_n, v_new)
        return s, o_n

    s_fin, o = lax.scan(step, s0.astype(F32), xs)
    return s_fin, jnp.moveaxis(o, 0, 2).reshape(b, h, l, dv)


def _gdn_direction(q, k, v, g, beta, s0, reverse):
    if reverse:
        q, k, v, g, beta = (jnp.flip(t, axis=2) for t in (q, k, v, g, beta))
    s_fin, o = _gdn_chunked(q, k, v, g, beta, s0)
    if reverse:
        o = jnp.flip(o, axis=2)
    return s_fin, o


def _gdn_gates(a_in, b_in, a_log, dt_bias):
    g = -jnp.exp(a_log.astype(F32)) * jax.nn.softplus(a_in.astype(F32) + dt_bias.astype(F32))
    beta = jax.nn.sigmoid(b_in.astype(F32))
    return jnp.swapaxes(g, 1, 2), jnp.swapaxes(beta, 1, 2)


def _conv_deltanet_mixer(h_l, h_c, w_in, conv_a, conv_qkv, a_log, dt_bias, gnorm, w_out, need_ctx):
    def prep(h):
        b, l, _ = h.shape
        xa, gb, gcv, qkv, z, af, bf, ab, bb = _split(h @ w_in, EVEN_SPLITS)
        qkv = jax.nn.silu(_dwconv3(qkv, conv_qkv))
        q, k, v = _split(qkv, (GDN_HEADS * GDN_DK, GDN_HEADS * GDN_DK, GDN_HEADS * GDN_DV))
        heads = lambda t: jnp.swapaxes(t.reshape(b, l, GDN_HEADS, -1), 1, 2)
        q = _l2norm(heads(q)) * (GDN_DK ** -0.5)
        k = _l2norm(heads(k))
        v = heads(v)
        gates = (_gdn_gates(af, bf, a_log[0], dt_bias[0]), _gdn_gates(ab, bb, a_log[1], dt_bias[1]))
        return (xa, gb, gcv), z, (q, k, v), gates

    a_l, z_l, qkv_l, gates_l = prep(h_l)
    a_c, z_c, qkv_c, gates_c = prep(h_c)
    s0 = jnp.zeros((h_l.shape[0], GDN_HEADS, GDN_DK, GDN_DV), F32)
    outs_l, outs_c = [], []
    for d in range(2):
        s_ctx, o_cd = _gdn_direction(*qkv_c, *gates_c[d], s0, d == 1)
        _, o_ld = _gdn_direction(*qkv_l, *gates_l[d], s_ctx, d == 1)
        outs_l.append(o_ld)
        outs_c.append(o_cd)

    def merge(a_parts, z, o):
        xa, gb, gcv = a_parts
        y_a = gb * _dwconv3(gcv * xa, conv_a)
        bsz, l, _ = z.shape
        o = jnp.swapaxes(o, 1, 2)
        y_b = _rmsnorm(o, gnorm) * jax.nn.silu(z.astype(F32)).reshape(bsz, l, GDN_HEADS, GDN_DV)
        y_b = y_b.reshape(bsz, l, GDN_HEADS * GDN_DV).astype(y_a.dtype)
        return jnp.concatenate([y_a, y_b], axis=-1) @ w_out

    y_l = merge(a_l, z_l, outs_l[0] + outs_l[1])
    y_c = merge(a_c, z_c, outs_c[0] + outs_c[1]) if need_ctx else None
    return y_l, y_c


def _attend(q, k, v):
    s = jnp.einsum('bqkgd,bskd->bkgqs', q, k, preferred_element_type=F32) * (ATTN_HEAD_DIM ** -0.5)
    p = jax.nn.softmax(s, axis=-1).astype(v.dtype)
    return jnp.einsum('bkgqs,bskd->bqkgd', p, v)


def _attention_mixer(h_l, h_c, w_qkv, q_norm, k_norm, w_o, cos, sin, need_ctx):
    def heads(h):
        b, l, _ = h.shape
        q, k, v = _split(h @ w_qkv, (ATTN_HEADS * ATTN_HEAD_DIM, ATTN_KV_HEADS * ATTN_HEAD_DIM, ATTN_KV_HEADS * ATTN_HEAD_DIM))
        q = _rmsnorm(q.reshape(b, l, ATTN_KV_HEADS, ATTN_GROUP, ATTN_HEAD_DIM), q_norm)
        k = _rmsnorm(k.reshape(b, l, ATTN_KV_HEADS, ATTN_HEAD_DIM), k_norm)
        v = v.reshape(b, l, ATTN_KV_HEADS, ATTN_HEAD_DIM)
        return q, k, v

    b, l, _ = h_l.shape
    q_l, k_l, v_l = heads(h_l)
    q_l = _rope(q_l, cos[:, None, None, :], sin[:, None, None, :])
    k_l = _rope(k_l, cos[:, None, :], sin[:, None, :])
    q_c, k_c, v_c = heads(h_c)
    k_all = jnp.concatenate([k_l, k_c], axis=1)
    v_all = jnp.concatenate([v_l, v_c], axis=1)
    nb = l // Q_BLOCK
    q_blocks = jnp.moveaxis(q_l.reshape(b, nb, Q_BLOCK, ATTN_KV_HEADS, ATTN_GROUP, ATTN_HEAD_DIM), 1, 0)
    o_l = lax.map(lambda qb: _attend(qb, k_all, v_all), q_blocks)
    y_l = jnp.moveaxis(o_l, 0, 1).reshape(b, l, ATTN_HEADS * ATTN_HEAD_DIM) @ w_o
    y_c = None
    if need_ctx:
        y_c = _attend(q_c, k_c, v_c).reshape(b, h_c.shape[1], ATTN_HEADS * ATTN_HEAD_DIM) @ w_o
    return y_l, y_c


def _hier_moe(h, w_group, w_expert, w_gate, w_up, w_down):
    t, d = h.shape
    g_logits = (h @ w_group).astype(F32)
    grp = jnp.argmax(g_logits, axis=-1).astype(jnp.int32)
    g_prob = jnp.take_along_axis(jax.nn.softmax(g_logits, axis=-1), grp[:, None], axis=1)
    e_logits = (h @ w_expert).astype(F32).reshape(t, N_GROUPS, EXPERTS_PER_GROUP)
    e_in = jnp.take_along_axis(e_logits, grp[:, None, None], axis=1)[:, 0]
    top_val, top_idx = lax.top_k(e_in, TOP_K)
    gate = jax.nn.softmax(top_val, axis=-1) * g_prob
    expert = grp[:, None] * EXPERTS_PER_GROUP + top_idx.astype(jnp.int32)
    tk = t * TOP_K
    flat_e = expert.reshape(tk)
    order = jnp.argsort(flat_e).astype(jnp.int32)
    se = flat_e[order]
    stok = order // TOP_K
    sw = gate.reshape(tk)[order]
    counts = jnp.zeros((N_EXPERTS,), jnp.int32).at[flat_e].add(1)
    starts = jnp.cumsum(counts) - counts
    padded = (counts + MOE_BLOCK - 1) // MOE_BLOCK * MOE_BLOCK
    pad_starts = jnp.cumsum(padded) - padded
    dest = pad_starts[se] + jnp.arange(tk, dtype=jnp.int32) - starts[se]
    n_blocks = -(-tk // MOE_BLOCK) + N_EXPERTS
    p = n_blocks * MOE_BLOCK
    src = jnp.full((p,), t, jnp.int32).at[dest].set(stok)
    wbuf = jnp.zeros((p,), h.dtype).at[dest].set(sw.astype(h.dtype))
    buf = jnp.concatenate([h, jnp.zeros((1, d), h.dtype)], axis=0)[src].reshape(n_blocks, MOE_BLOCK, d)
    blk_start = jnp.arange(n_blocks, dtype=jnp.int32) * MOE_BLOCK
    blk_expert = jnp.minimum(jnp.searchsorted(pad_starts + padded, blk_start, side='right'), N_EXPERTS - 1)

    def expert_block(args):
        xb, e = args
        return (jax.nn.silu(xb @ w_gate[e]) * (xb @ w_up[e])) @ w_down[e]

    ybuf = lax.map(expert_block, (buf, blk_expert)).reshape(p, d)
    return jax.ops.segment_sum(ybuf * wbuf[:, None], src, num_segments=t + 1)[:t]


def setup_inputs(seed: int = 0) -> dict:
    key = jax.random.key(seed)
    ks = iter(jax.random.split(key, 40))
    nrm = lambda shape, scale: jax.random.normal(next(ks), shape, jnp.float32) * scale
    d = D_MODEL
    ne = (DEPTH + 1) // 2
    no = DEPTH // 2
    x = nrm((BATCH, SEQ, d), 1.0)
    c = nrm((BATCH, d), 1.0)
    ctx = nrm((BATCH, CTX_LEN, d), 1.0)
    c_ctx = nrm((d,), 1.0)
    mod_w = nrm((DEPTH, d, 6 * d), 0.5 * d ** -0.5)
    mod_b = nrm((DEPTH, 6 * d), 0.02)
    norm1 = 1.0 + nrm((DEPTH, d), 0.02)
    norm2 = 1.0 + nrm((DEPTH, d), 0.02)
    ab_w_in = nrm((ne, d, EVEN_IN), d ** -0.5)
    ab_conv_a = nrm((ne, 3, A_WIDTH), 3 ** -0.5)
    ab_conv_qkv = nrm((ne, 3, QKV_WIDTH), 3 ** -0.5)
    ab_a_log = jnp.log(jax.random.uniform(next(ks), (ne, 2, GDN_HEADS), jnp.float32, 1.0, 16.0))
    dt = jnp.exp(jax.random.uniform(next(ks), (ne, 2, GDN_HEADS), jnp.float32, math.log(1e-3), math.log(1e-1)))
    ab_dt_bias = dt + jnp.log(-jnp.expm1(-dt))
    ab_gnorm = 1.0 + nrm((ne, GDN_DV), 0.02)
    ab_w_out = nrm((ne, EVEN_OUT, d), EVEN_OUT ** -0.5)
    attn_w_qkv = nrm((no, d, ODD_IN), d ** -0.5)
    attn_q_norm = 1.0 + nrm((no, ATTN_HEAD_DIM), 0.02)
    attn_k_norm = 1.0 + nrm((no, ATTN_HEAD_DIM), 0.02)
    attn_w_o = nrm((no, ATTN_HEADS * ATTN_HEAD_DIM, d), (ATTN_HEADS * ATTN_HEAD_DIM) ** -0.5)
    moe_w_group = nrm((DEPTH, d, N_GROUPS), d ** -0.5)
    moe_w_expert = nrm((DEPTH, d, N_EXPERTS), d ** -0.5)
    moe_w_gate = nrm((DEPTH, N_EXPERTS, d, D_EXPERT), d ** -0.5)
    moe_w_up = nrm((DEPTH, N_EXPERTS, d, D_EXPERT), d ** -0.5)
    moe_w_down = nrm((DEPTH, N_EXPERTS, D_EXPERT, d), D_EXPERT ** -0.5)
    final_norm = 1.0 + nrm((d,), 0.02)
    return {'x': x, 'c': c, 'ctx': ctx, 'c_ctx': c_ctx, 'mod_w': mod_w, 'mod_b': mod_b,
            'norm1': norm1, 'norm2': norm2, 'ab_w_in': ab_w_in, 'ab_conv_a': ab_conv_a,
            'ab_conv_qkv': ab_conv_qkv, 'ab_a_log': ab_a_log, 'ab_dt_bias': ab_dt_bias,
            'ab_gnorm': ab_gnorm, 'ab_w_out': ab_w_out, 'attn_w_qkv': attn_w_qkv,
            'attn_q_norm': attn_q_norm, 'attn_k_norm': attn_k_norm, 'attn_w_o': attn_w_o,
            'moe_w_group': moe_w_group, 'moe_w_expert': moe_w_expert, 'moe_w_gate': moe_w_gate,
            'moe_w_up': moe_w_up, 'moe_w_down': moe_w_down, 'final_norm': final_norm}


def reference(x, c, ctx, c_ctx, mod_w, mod_b, norm1, norm2, ab_w_in, ab_conv_a, ab_conv_qkv,
              ab_a_log, ab_dt_bias, ab_gnorm, ab_w_out, attn_w_qkv, attn_q_norm, attn_k_norm,
              attn_w_o, moe_w_group, moe_w_expert, moe_w_gate, moe_w_up, moe_w_down, final_norm):
    b, l, d = x.shape
    n_ctx = ctx.shape[1]
    ROWS = l // GRID_W
    rows = jnp.repeat(jnp.arange(ROWS, dtype=jnp.int32), GRID_W)
    cols = jnp.tile(jnp.arange(GRID_W, dtype=jnp.int32), ROWS)
    cos, sin = _axial_rope(rows, cols)
    cond_l = jax.nn.silu(c)
    cond_c = jax.nn.silu(c_ctx)
    for i in range(DEPTH):
        need_ctx = i < DEPTH - 1
        j = i // 2
        mod_l = (cond_l @ mod_w[i] + mod_b[i])[:, None, :]
        mod_c = cond_c @ mod_w[i] + mod_b[i]
        sh1, sc1, g1, sh2, sc2, g2 = jnp.split(mod_l, 6, axis=-1)
        csh1, csc1, cg1, csh2, csc2, cg2 = jnp.split(mod_c, 6, axis=-1)
        h_l = _rmsnorm(x, norm1[i]) * (1.0 + sc1) + sh1
        h_c = _rmsnorm(ctx, norm1[i]) * (1.0 + csc1) + csh1
        if i % 2 == 0:
            y_l, y_c = _conv_deltanet_mixer(h_l, h_c, ab_w_in[j], ab_conv_a[j], ab_conv_qkv[j], ab_a_log[j],
                                            ab_dt_bias[j], ab_gnorm[j], ab_w_out[j], need_ctx)
        else:
            y_l, y_c = _attention_mixer(h_l, h_c, attn_w_qkv[j], attn_q_norm[j], attn_k_norm[j], attn_w_o[j],
                                        cos, sin, need_ctx)
        x = x + g1 * y_l
        h2_l = (_rmsnorm(x, norm2[i]) * (1.0 + sc2) + sh2).reshape(b * l, d)
        if need_ctx:
            ctx = ctx + cg1 * y_c
            h2_c = (_rmsnorm(ctx, norm2[i]) * (1.0 + csc2) + csh2).reshape(b * n_ctx, d)
            tokens = jnp.concatenate([h2_l, h2_c], axis=0)
        else:
            tokens = h2_l
        y = _hier_moe(tokens, moe_w_group[i], moe_w_expert[i], moe_w_gate[i], moe_w_up[i], moe_w_down[i])
        x = x + g2 * y[:b * l].reshape(b, l, d)
        if need_ctx:
            ctx = ctx + cg2 * y[b * l:].reshape(b, n_ctx, d)
    return _rmsnorm(x, final_norm)
```

```python
import functools

import jax
import jax.numpy as jnp
from jax import lax
from jax.experimental import pallas as pl
from jax.experimental.pallas import tpu as pltpu

F32 = jnp.float32
BF16 = jnp.bfloat16
HI = lax.Precision.HIGHEST

D_MODEL = 1024
RMS_EPS = 1e-6
L2_EPS = 1e-6
GRID_W = 64
A_WIDTH = 512
GDN_HEADS = 8
GDN_DK = 64
GDN_DV = 64
GDN_CHUNK = 64
QKV_WIDTH = 1536
ATTN_HEADS = 8
ATTN_KV_HEADS = 2
ATTN_GROUP = 4
ATTN_HEAD_DIM = 128
ROPE_THETA = 10000.0
N_GROUPS = 8
EXPERTS_PER_GROUP = 8
N_EXPERTS = 64
D_EXPERT = 384
MOE_BLOCK = 256

LANES = 128
GDN_PAIRS = GDN_HEADS // 2
SUPER = 2 * GDN_CHUNK
VMEM_LIMIT = 52 * 1024 * 1024


def _cparams(sem):
    return pltpu.CompilerParams(dimension_semantics=sem, vmem_limit_bytes=VMEM_LIMIT)


def _silu(x):
    return x * jax.nn.sigmoid(x)


def _norm_mod(x, nw, sc, sh):
    ms = jnp.mean(x * x, axis=-1, keepdims=True)
    return (x * lax.rsqrt(ms + RMS_EPS)) * nw * (1.0 + sc) + sh


def _bdot(a, b):
    return jnp.dot(a.astype(BF16), b.astype(BF16), preferred_element_type=F32)


def _bdot_nt(a, b):
    return lax.dot_general(a.astype(BF16), b.astype(BF16), (((1,), (1,)), ((), ())),
                           preferred_element_type=F32)


def _bdot_tn(a, b):
    return lax.dot_general(a.astype(BF16), b.astype(BF16), (((0,), (0,)), ((), ())),
                           preferred_element_type=F32)


def _mod_kernel(c_ref, w_ref, b_ref, o_ref):
    c = c_ref[...]
    o_ref[0] = jnp.dot(_silu(c), w_ref[0], precision=HI, preferred_element_type=F32) + b_ref[0]


def _modulation(cond, mod_w, mod_b):
    depth, d, n = mod_w.shape
    r = cond.shape[0]
    tn = 512
    return pl.pallas_call(
        _mod_kernel, grid=(depth, n // tn),
        in_specs=[pl.BlockSpec((r, d), lambda i, j: (0, 0)),
                  pl.BlockSpec((1, d, tn), lambda i, j: (i, 0, j)),
                  pl.BlockSpec((1, 1, tn), lambda i, j: (i, 0, j))],
        out_specs=pl.BlockSpec((1, r, tn), lambda i, j: (i, 0, j)),
        out_shape=jax.ShapeDtypeStruct((depth, r, n), F32),
        compiler_params=_cparams(("arbitrary", "arbitrary")),
        name="modulation")(cond, mod_w, mod_b.reshape(depth, 1, n))


P0_A = 3 * A_WIDTH
P0_Z = GDN_HEADS * GDN_DV
P0_COLS = P0_A + QKV_WIDTH + P0_Z + LANES


def _proj0_kernel(x_ref, nw_ref, sc_ref, sh_ref, w_ref, a_ref, qkv_ref, z_ref, g_ref):
    h = _norm_mod(x_ref[0], nw_ref[...], sc_ref[0], sh_ref[0]).astype(BF16)
    c0, c1, c2 = P0_A, P0_A + QKV_WIDTH, P0_A + QKV_WIDTH + P0_Z
    a_ref[0] = jnp.dot(h, w_ref[:, 0:c0], preferred_element_type=F32).astype(BF16)
    qkv_ref[0] = jnp.dot(h, w_ref[:, c0:c1], preferred_element_type=F32).astype(BF16)
    z_ref[0] = jnp.dot(h, w_ref[:, c1:c2], preferred_element_type=F32).astype(BF16)
    g_ref[0] = jnp.dot(h, w_ref[:, c2:], preferred_element_type=F32)


def _proj0(x, nw, sc, sh, w, tm):
    b, l, d = x.shape
    per_batch = sc.shape[0] > 1
    mod_map = (lambda i, j: (i, 0, 0)) if per_batch else (lambda i, j: (0, 0, 0))
    row = lambda i, j: (i, j, 0)
    return pl.pallas_call(
        _proj0_kernel, grid=(b, l // tm),
        in_specs=[pl.BlockSpec((1, tm, d), row),
                  pl.BlockSpec((1, d), lambda i, j: (0, 0)),
                  pl.BlockSpec((1, 1, d), mod_map),
                  pl.BlockSpec((1, 1, d), mod_map),
                  pl.BlockSpec((d, P0_COLS), lambda i, j: (0, 0))],
        out_specs=[pl.BlockSpec((1, tm, P0_A), row),
                   pl.BlockSpec((1, tm, QKV_WIDTH), row),
                   pl.BlockSpec((1, tm, P0_Z), row),
                   pl.BlockSpec((1, tm, LANES), row)],
        out_shape=[jax.ShapeDtypeStruct((b, l, P0_A), BF16),
                   jax.ShapeDtypeStruct((b, l, QKV_WIDTH), BF16),
                   jax.ShapeDtypeStruct((b, l, P0_Z), BF16),
                   jax.ShapeDtypeStruct((b, l, LANES), F32)],
        compiler_params=_cparams(("arbitrary", "arbitrary")),
        name="proj0")(x, nw, sc, sh, w)


PRE_TILE = 256


def _gdn_kernel(qc_ref, kc_ref, vc_ref, ql_ref, kl_ref, vl_ref,
                xac_ref, gbc_ref, gcc_ref, xal_ref, gbl_ref, gcl_ref,
                zc_ref, zl_ref, gtc_ref, gtl_ref,
                cwq_ref, cwk_ref, cwv_ref, cwa_ref, a8_ref, dt8_ref, gn_ref,
                yac_ref, ybc_ref, yal_ref, ybl_ref,
                qn_s, kn_s, vn_s, g_s, bt_s, u_s, w_s, qg_s, kg_s, att_s, gl_s, o_s,
                *, n_ctx, n_lat):
    n_tot = n_ctx + n_lat
    lane = lax.broadcasted_iota(jnp.int32, (1, LANES), 1)
    lo = lane < GDN_DK
    head_blk = (lax.broadcasted_iota(jnp.int32, (LANES, LANES), 0) // GDN_DK
                == lax.broadcasted_iota(jnp.int32, (LANES, LANES), 1) // GDN_DK)
    blk_ones = jnp.where(head_blk, 1.0, 0.0).astype(F32)

    def conv3_tile(ref, r0, n, w_ref, mul_ref=None):
        def rows(start, size):
            v = ref[0, pl.ds(start, size), :].astype(F32)
            if mul_ref is not None:
                v = v * mul_ref[0, pl.ds(start, size), :].astype(F32)
            return v
        cur = rows(r0, PRE_TILE)
        prev = rows(pl.multiple_of(jnp.maximum(r0 - 16, 0), 16), 16)[15:16, :] * (r0 > 0).astype(F32)
        nxt = (rows(pl.multiple_of(jnp.minimum(r0 + PRE_TILE, n - 16), 16), 16)[0:1, :]
               * (r0 + PRE_TILE < n).astype(F32))
        row = lax.broadcasted_iota(jnp.int32, (PRE_TILE, 1), 0)
        xm = jnp.where(row == 0, prev, pltpu.roll(cur, 1, 0))
        xp = jnp.where(row == PRE_TILE - 1, nxt, pltpu.roll(cur, PRE_TILE - 1, 0))
        return xm * w_ref[0:1, :] + cur * w_ref[1:2, :] + xp * w_ref[2:3, :]

    def head_sumsq(x):
        return jnp.dot(x * x, blk_ones, precision=HI, preferred_element_type=F32)

    def prepass(seg_off, n, q_ref, k_ref, v_ref, gt_ref):
        def body(t, carry):
            r0 = pl.multiple_of(t * PRE_TILE, PRE_TILE)
            dst = pl.ds(pl.multiple_of(seg_off + r0, PRE_TILE), PRE_TILE)
            q = _silu(conv3_tile(q_ref, r0, n, cwq_ref))
            k = _silu(conv3_tile(k_ref, r0, n, cwk_ref))
            v = _silu(conv3_tile(v_ref, r0, n, cwv_ref))
            qn_s[dst, :] = q * lax.rsqrt(head_sumsq(q) + L2_EPS) * (GDN_DK ** -0.5)
            kn_s[dst, :] = k * lax.rsqrt(head_sumsq(k) + L2_EPS)
            vn_s[dst, :] = v
            gt = gt_ref[0, 0, pl.ds(r0, PRE_TILE), :]
            sp_in = gt + dt8_ref[0]
            softplus = jnp.maximum(sp_in, 0.0) + jnp.log1p(jnp.exp(-jnp.abs(sp_in)))
            graw = -jnp.exp(a8_ref[0]) * softplus
            beta = jax.nn.sigmoid(gt)
            for d in range(2):
                c = 4 * d
                g_s[d, dst, :] = jnp.where(lo, graw[:, c:c + 1], graw[:, c + 1:c + 2])
                bt_s[d, dst, :] = jnp.where(lo, beta[:, c + 2:c + 3], beta[:, c + 3:c + 4])
            return carry
        lax.fori_loop(0, n // PRE_TILE, body, 0)

    prepass(0, n_ctx, qc_ref, kc_ref, vc_ref, gtc_ref)
    prepass(n_ctx, n_lat, ql_ref, kl_ref, vl_ref, gtl_ref)

    ii = lax.broadcasted_iota(jnp.int32, (SUPER, SUPER), 0)
    jj = lax.broadcasted_iota(jnp.int32, (SUPER, SUPER), 1)
    xr = ii ^ jj
    same_chunk = xr < GDN_CHUNK
    top = lax.broadcasted_iota(jnp.int32, (SUPER, 1), 0) < GDN_CHUNK
    eye = jnp.where(ii == jj, 1.0, 0.0).astype(F32)
    cs_all = jnp.where(same_chunk, 1.0, 0.0).astype(F32)

    def phase1(s, carry):
        r0 = pl.multiple_of(s * SUPER, SUPER)
        rows = pl.ds(r0, SUPER)
        q_s = qn_s[rows, :]
        k_s = kn_s[rows, :]
        v_s = vn_s[rows, :]
        kb = k_s.astype(BF16)
        gram, qk = [], []
        for hh in range(2):
            hm = lo if hh == 0 else jnp.logical_not(lo)
            gram.append(_bdot_nt(jnp.where(hm, k_s, 0.0), kb))
            qk.append(_bdot_nt(jnp.where(hm, q_s, 0.0), kb))
        for d in range(2):
            tri = (jj <= ii) if d == 0 else (jj >= ii)
            valid = jnp.logical_and(same_chunk, tri)
            g_l = g_s[d, rows, :]
            b_l = bt_s[d, rows, :]
            gc = jnp.dot(jnp.where(valid, 1.0, 0.0).astype(F32), g_l, precision=HI,
                         preferred_element_type=F32)
            gcl = jnp.dot(cs_all, g_l, precision=HI, preferred_element_type=F32)
            egc = jnp.exp(gc)
            qg_s[d, rows, :] = (q_s * egc).astype(BF16)
            kg_s[d, rows, :] = (k_s * jnp.exp(gcl - gc)).astype(BF16)
            gl_s[d, rows, :] = jnp.exp(gcl)
            rhs = jnp.concatenate([v_s * b_l, k_s * b_l * egc], axis=1).astype(BF16)
            gct = gc.T
            uw, att = [], []
            for hh in range(2):
                c0 = hh * GDN_DK
                diff = gc[:, c0:c0 + 1] - gct[c0:c0 + 1, :]
                dec = jnp.exp(jnp.where(valid, diff, -jnp.inf))
                nm = jnp.where(ii == jj, 0.0, b_l[:, c0:c0 + 1] * gram[hh] * dec)
                att.append(qk[hh] * dec)
                t_inv = eye - jnp.where(xr < 2, nm, 0.0)
                for lvl in range(1, 6):
                    off = jnp.where((xr >> lvl) == 1, nm, 0.0)
                    t_inv = t_inv - _bdot(t_inv, _bdot(off, t_inv))
                uw.append(_bdot(t_inv, rhs))
            u_s[d, rows, :] = jnp.where(lo, uw[0][:, :LANES], uw[1][:, :LANES])
            w_s[d, rows, :] = jnp.where(lo, uw[0][:, LANES:], uw[1][:, LANES:]).astype(BF16)
            sw0 = pltpu.roll(att[0], GDN_CHUNK, 1)
            sw1 = pltpu.roll(att[1], GDN_CHUNK, 1)
            att_s[d, rows, :] = jnp.where(top, jnp.where(lo, att[0], sw1),
                                          jnp.where(lo, sw0, att[1])).astype(BF16)
        return carry

    lax.fori_loop(0, n_tot // SUPER, phase1, 0)

    n_ch = n_tot // GDN_CHUNK
    n_cc = n_ctx // GDN_CHUNK
    bd_mask = head_blk

    def scan_step(n, states):
        c_f = n
        c_b = jnp.where(n < n_cc, n_cc - 1 - n, n_ch - 1 + n_cc - n)
        new_states = []
        for d, c in ((0, c_f), (1, c_b)):
            st = states[d]
            rows = pl.ds(pl.multiple_of(c * GDN_CHUNK, GDN_CHUNK), GDN_CHUNK)
            lhs = jnp.concatenate([w_s[d, rows, :], qg_s[d, rows, :]], axis=0)
            wq = jnp.dot(lhs, st.astype(BF16), preferred_element_type=F32)
            v_new = u_s[d, rows, :] - wq[:GDN_CHUNK]
            vb = v_new.astype(BF16)
            zero = jnp.zeros_like(vb)
            bdv = jnp.concatenate([jnp.where(lo, vb, zero), jnp.where(lo, zero, vb)], axis=0)
            o_s[d, rows, :] = wq[GDN_CHUNK:] + jnp.dot(att_s[d, rows, :], bdv, preferred_element_type=F32)
            kv = _bdot_tn(kg_s[d, rows, :], vb)
            gl = gl_s[d, pl.ds(pl.multiple_of(c * GDN_CHUNK, GDN_CHUNK), 1), :]
            new_states.append(st * gl + jnp.where(bd_mask, kv, 0.0))
        return tuple(new_states)

    s0 = jnp.zeros((LANES, LANES), F32)
    lax.fori_loop(0, n_ch, scan_step, (s0, s0))

    def postpass(seg_off, n, z_ref, xa_ref, gb_ref, gcv_ref, ya_ref, yb_ref):
        def body(t, carry):
            r0 = pl.multiple_of(t * PRE_TILE, PRE_TILE)
            src = pl.ds(pl.multiple_of(seg_off + r0, PRE_TILE), PRE_TILE)
            dst = pl.ds(r0, PRE_TILE)
            o = o_s[0, src, :] + o_s[1, src, :]
            ms = head_sumsq(o) * (1.0 / GDN_DV)
            z = z_ref[0, dst, :].astype(F32)
            yb_ref[0, dst, :] = ((o * lax.rsqrt(ms + RMS_EPS)) * gn_ref[...] * _silu(z)).astype(BF16)
            conv = conv3_tile(gcv_ref, r0, n, cwa_ref, mul_ref=xa_ref)
            ya_ref[0, dst, :] = (gb_ref[0, dst, :].astype(F32) * conv).astype(BF16)
            return carry
        lax.fori_loop(0, n // PRE_TILE, body, 0)

    postpass(0, n_ctx, zc_ref, xac_ref, gbc_ref, gcc_ref, yac_ref, ybc_ref)
    postpass(n_ctx, n_lat, zl_ref, xal_ref, gbl_ref, gcl_ref, yal_ref, ybl_ref)


def _gdn_mixer(a_c, qkv_c, z_c, gt_c, a_l, qkv_l, z_l, gt_l, conv_qkv, conv_a, a8, dt8, gnorm2):
    b, n_ctx, _ = qkv_c.shape
    n_lat = qkv_l.shape[1]
    n_tot = n_ctx + n_lat
    np_ = GDN_PAIRS

    def seq(n, off):
        return pl.BlockSpec((1, n, LANES), lambda i, p, off=off: (i, 0, off + p))

    def cw(off):
        return pl.BlockSpec((3, LANES), lambda i, p, off=off: (0, off + p))

    gate = lambda n: pl.BlockSpec((1, 1, n, 8), lambda i, p: (i, p, 0, 0))
    vec8 = pl.BlockSpec((1, 1, 8), lambda i, p: (p, 0, 0))
    in_specs = [seq(n_ctx, 0), seq(n_ctx, np_), seq(n_ctx, 2 * np_),
                seq(n_lat, 0), seq(n_lat, np_), seq(n_lat, 2 * np_),
                seq(n_ctx, 0), seq(n_ctx, np_), seq(n_ctx, 2 * np_),
                seq(n_lat, 0), seq(n_lat, np_), seq(n_lat, 2 * np_),
                seq(n_ctx, 0), seq(n_lat, 0), gate(n_ctx), gate(n_lat),
                cw(0), cw(np_), cw(2 * np_), cw(0), vec8, vec8,
                pl.BlockSpec((1, LANES), lambda i, p: (0, 0))]
    out_specs = [seq(n_ctx, 0), seq(n_ctx, 0), seq(n_lat, 0), seq(n_lat, 0)]
    out_shape = [jax.ShapeDtypeStruct((b, n_ctx, A_WIDTH), BF16),
                 jax.ShapeDtypeStruct((b, n_ctx, GDN_HEADS * GDN_DV), BF16),
                 jax.ShapeDtypeStruct((b, n_lat, A_WIDTH), BF16),
                 jax.ShapeDtypeStruct((b, n_lat, GDN_HEADS * GDN_DV), BF16)]
    f32buf = lambda *lead: pltpu.VMEM((*lead, n_tot, LANES), F32)
    b16buf = lambda *lead: pltpu.VMEM((*lead, n_tot, LANES), BF16)
    scratch = [f32buf(), f32buf(), f32buf(), f32buf(2), f32buf(2), f32buf(2),
               b16buf(2), b16buf(2), b16buf(2), b16buf(2), f32buf(2), f32buf(2)]
    return pl.pallas_call(
        functools.partial(_gdn_kernel, n_ctx=n_ctx, n_lat=n_lat),
        grid=(b, np_), in_specs=in_specs, out_specs=out_specs, out_shape=out_shape,
        scratch_shapes=scratch,
        compiler_params=_cparams(("arbitrary", "arbitrary")),
        name="gdn_mixer")(qkv_c, qkv_c, qkv_c, qkv_l, qkv_l, qkv_l,
                          a_c, a_c, a_c, a_l, a_l, a_l, z_c, z_l, gt_c, gt_l,
                          conv_qkv, conv_qkv, conv_qkv, conv_a, a8, dt8, gnorm2)


def _route(logits):
    lane = lax.broadcasted_iota(jnp.int32, logits.shape, 1)
    lane_f = lane.astype(F32)
    neg = -jnp.inf
    is_g = lane < N_GROUPS
    gl = jnp.where(is_g, logits, neg)
    gmax = jnp.max(gl, axis=-1, keepdims=True)
    grp = jnp.min(jnp.where(gl == gmax, lane_f, float(LANES)), axis=-1, keepdims=True)
    gsum = jnp.sum(jnp.where(is_g, jnp.exp(gl - gmax), 0.0), axis=-1, keepdims=True)
    gprob = 1.0 / gsum
    first = float(N_GROUPS) + grp * float(EXPERTS_PER_GROUP)
    ing = jnp.logical_and(lane_f >= first, lane_f < first + float(EXPERTS_PER_GROUP))
    el = jnp.where(ing, logits, neg)
    v1 = jnp.max(el, axis=-1, keepdims=True)
    i1 = jnp.min(jnp.where(el == v1, lane_f, float(LANES)), axis=-1, keepdims=True)
    el2 = jnp.where(lane_f == i1, neg, el)
    v2 = jnp.max(el2, axis=-1, keepdims=True)
    i2 = jnp.min(jnp.where(el2 == v2, lane_f, float(LANES)), axis=-1, keepdims=True)
    t = jnp.exp(v2 - v1)
    den = 1.0 + t
    w1 = (1.0 / den) * gprob
    w2 = (t / den) * gprob
    e_out = jnp.where(lane == 0, i1 - float(N_GROUPS), jnp.where(lane == 1, i2 - float(N_GROUPS), 0.0))
    w_out = jnp.where(lane == 0, w1, jnp.where(lane == 1, w2, 0.0))
    return e_out.astype(jnp.int32), w_out


def _mix_out_kernel(*refs, n_parts):
    ys = refs[:n_parts]
    ws = refs[n_parts:2 * n_parts]
    x_ref, g1_ref, nw_ref, sc_ref, sh_ref, wr_ref = refs[2 * n_parts:2 * n_parts + 6]
    xo_ref, h2_ref, re_ref, rw_ref = refs[2 * n_parts + 6:]
    y = jnp.dot(ys[0][0], ws[0][...], preferred_element_type=F32)
    for k in range(1, n_parts):
        y = y + jnp.dot(ys[k][0], ws[k][...], preferred_element_type=F32)
    x = x_ref[0] + g1_ref[0] * y
    xo_ref[0] = x
    h2 = _norm_mod(x, nw_ref[...], sc_ref[0], sh_ref[0])
    h2_ref[0] = h2
    logits = jnp.dot(h2, wr_ref[...], precision=HI, preferred_element_type=F32)
    e_out, w_out = _route(logits)
    re_ref[0] = e_out
    rw_ref[0] = w_out


def _mix_out(parts, weights, x, g1, nw, sc, sh, w_router, tm):
    b, l, d = x.shape
    n_parts = len(parts)
    per_batch = g1.shape[0] > 1
    mod_map = (lambda i, j: (i, 0, 0)) if per_batch else (lambda i, j: (0, 0, 0))
    row = lambda i, j: (i, j, 0)
    const2 = lambda i, j: (0, 0)
    in_specs = ([pl.BlockSpec((1, tm, p.shape[2]), row) for p in parts]
                + [pl.BlockSpec(w.shape, const2) for w in weights]
                + [pl.BlockSpec((1, tm, d), row), pl.BlockSpec((1, 1, d), mod_map),
                   pl.BlockSpec((1, d), const2), pl.BlockSpec((1, 1, d), mod_map),
                   pl.BlockSpec((1, 1, d), mod_map), pl.BlockSpec((d, LANES), const2)])
    return pl.pallas_call(
        functools.partial(_mix_out_kernel, n_parts=n_parts), grid=(b, l // tm),
        in_specs=in_specs,
        out_specs=[pl.BlockSpec((1, tm, d), row), pl.BlockSpec((1, tm, d), row),
                   pl.BlockSpec((1, tm, LANES), row), pl.BlockSpec((1, tm, LANES), row)],
        out_shape=[jax.ShapeDtypeStruct((b, l, d), F32), jax.ShapeDtypeStruct((b, l, d), F32),
                   jax.ShapeDtypeStruct((b, l, LANES), jnp.int32),
                   jax.ShapeDtypeStruct((b, l, LANES), F32)],
        compiler_params=_cparams(("arbitrary", "arbitrary")),
        name="mix_out")(*parts, *weights, x, g1, nw, sc, sh, w_router)


def _dispatch(e, t):
    tk = 2 * t
    flat_e = e.reshape(tk)
    order = jnp.argsort(flat_e).astype(jnp.int32)
    se = flat_e[order]
    counts = jnp.zeros((N_EXPERTS,), jnp.int32).at[flat_e].add(1)
    starts = jnp.cumsum(counts) - counts
    padded = (counts + MOE_BLOCK - 1) // MOE_BLOCK * MOE_BLOCK
    pad_ends = jnp.cumsum(padded)
    pad_starts = pad_ends - padded
    dest = pad_starts[se] + jnp.arange(tk, dtype=jnp.int32) - starts[se]
    n_blocks = -(-tk // MOE_BLOCK) + N_EXPERTS
    p = n_blocks * MOE_BLOCK
    src = jnp.zeros((p,), jnp.int32).at[dest].set(order // 2)
    slot_of = jnp.zeros((tk,), jnp.int32).at[order].set(dest).reshape(t, 2)
    blk_start = jnp.arange(n_blocks, dtype=jnp.int32) * MOE_BLOCK
    blk_expert = jnp.minimum(jnp.searchsorted(pad_ends, blk_start, side='right'),
                             N_EXPERTS - 1).astype(jnp.int32)
    blk_valid = (blk_start < pad_ends[-1]).astype(jnp.int32)
    return src.reshape(n_blocks, 1, MOE_BLOCK), slot_of, blk_expert, blk_valid


def _gather_rows(idx_vmem_ref, idx_smem, slot, src_hbm, dst_buf, isem, gsem, n_rows):
    cp = pltpu.make_async_copy(idx_vmem_ref.at[0], idx_smem.at[pl.ds(slot, 1)], isem.at[0])
    cp.start()
    cp.wait()

    def body(r, carry):
        tok = idx_smem[slot, r]
        pltpu.make_async_copy(src_hbm.at[pl.ds(tok, 1)], dst_buf.at[slot, pl.ds(r, 1)],
                              gsem.at[slot]).start()
        return carry
    lax.fori_loop(0, n_rows, body, 0, unroll=8)


def _wait_rows(src_hbm, dst_buf, slot, gsem, n_rows):
    def body(r, carry):
        pltpu.make_async_copy(src_hbm.at[pl.ds(0, 1)], dst_buf.at[slot, pl.ds(r, 1)],
                              gsem.at[slot]).wait()
        return carry
    lax.fori_loop(0, n_rows, body, 0, unroll=8)


def _expert_kernel(be_ref, bv_ref, idx_ref, idx_nxt_ref, h_hbm, wg_ref, wu_ref, wd_ref, out_ref,
                   xbuf, idx_smem, gsem, isem):
    i = pl.program_id(0)
    n = pl.num_programs(0)
    slot = i % 2
    valid = bv_ref[i] > 0

    @pl.when(jnp.logical_and(i == 0, valid))
    def _():
        _gather_rows(idx_ref, idx_smem, 0, h_hbm, xbuf, isem, gsem, MOE_BLOCK)

    nxt = jnp.minimum(i + 1, n - 1)

    @pl.when(jnp.logical_and(i + 1 < n, bv_ref[nxt] > 0))
    def _():
        _gather_rows(idx_nxt_ref, idx_smem, 1 - slot, h_hbm, xbuf, isem, gsem, MOE_BLOCK)

    @pl.when(valid)
    def _():
        _wait_rows(h_hbm, xbuf, slot, gsem, MOE_BLOCK)
        x = xbuf[slot].astype(BF16)
        g = jnp.dot(x, wg_ref[0].astype(BF16), preferred_element_type=F32)
        u = jnp.dot(x, wu_ref[0].astype(BF16), preferred_element_type=F32)
        h = (_silu(g) * u).astype(BF16)
        out_ref[...] = jnp.dot(h, wd_ref[0].astype(BF16), preferred_element_type=F32)

    @pl.when(jnp.logical_not(valid))
    def _():
        out_ref[...] = jnp.zeros_like(out_ref)


def _experts(h2, src, blk_expert, blk_valid, w_gate, w_up, w_down):
    t, d = h2.shape
    n_blocks = src.shape[0]
    grid_spec = pltpu.PrefetchScalarGridSpec(
        num_scalar_prefetch=2, grid=(n_blocks,),
        in_specs=[pl.BlockSpec((1, 1, MOE_BLOCK), lambda i, be, bv: (i, 0, 0)),
                  pl.BlockSpec((1, 1, MOE_BLOCK), lambda i, be, bv: (jnp.minimum(i + 1, n_blocks - 1), 0, 0)),
                  pl.BlockSpec(memory_space=pl.ANY),
                  pl.BlockSpec((1, d, D_EXPERT), lambda i, be, bv: (be[i], 0, 0)),
                  pl.BlockSpec((1, d, D_EXPERT), lambda i, be, bv: (be[i], 0, 0)),
                  pl.BlockSpec((1, D_EXPERT, d), lambda i, be, bv: (be[i], 0, 0))],
        out_specs=pl.BlockSpec((MOE_BLOCK, d), lambda i, be, bv: (i, 0)),
        scratch_shapes=[pltpu.VMEM((2, MOE_BLOCK, d), F32),
                        pltpu.SMEM((2, MOE_BLOCK), jnp.int32),
                        pltpu.SemaphoreType.DMA((2,)),
                        pltpu.SemaphoreType.DMA((1,))])
    return pl.pallas_call(
        _expert_kernel, grid_spec=grid_spec,
        out_shape=jax.ShapeDtypeStruct((n_blocks * MOE_BLOCK, d), F32),
        compiler_params=_cparams(("arbitrary",)),
        name="experts")(blk_expert, blk_valid, src, src, h2, w_gate, w_up, w_down)


def _combine_kernel(idx_ref, idx_nxt_ref, y_hbm, x_ref, g2_ref, rw_ref, fn_ref, out_ref,
                    rbuf, idx_smem, gsem, isem, *, tm, final_norm):
    i = pl.program_id(0)
    n = pl.num_programs(0)
    slot = i % 2

    @pl.when(i == 0)
    def _():
        _gather_rows(idx_ref, idx_smem, 0, y_hbm, rbuf, isem, gsem, 2 * tm)

    @pl.when(i + 1 < n)
    def _():
        _gather_rows(idx_nxt_ref, idx_smem, 1 - slot, y_hbm, rbuf, isem, gsem, 2 * tm)

    _wait_rows(y_hbm, rbuf, slot, gsem, 2 * tm)
    rw = rw_ref[...]
    y = rw[:, 0:1] * rbuf[slot, 0:tm, :] + rw[:, 1:2] * rbuf[slot, tm:2 * tm, :]
    x = x_ref[...] + g2_ref[0] * y
    if final_norm:
        ms = jnp.mean(x * x, axis=-1, keepdims=True)
        x = (x * lax.rsqrt(ms + RMS_EPS)) * fn_ref[...]
    out_ref[...] = x


def _combine(ybuf, slot_of, x, g2, rw, fn, rows_per_batch, tm, final_norm):
    t, d = x.shape
    nt = t // tm
    idx = jnp.concatenate([slot_of[:, 0].reshape(nt, 1, tm), slot_of[:, 1].reshape(nt, 1, tm)], axis=2)
    per_batch = g2.shape[0] > 1
    tpb = rows_per_batch // tm
    mod_map = (lambda i: (i // tpb, 0, 0)) if per_batch else (lambda i: (0, 0, 0))
    return pl.pallas_call(
        functools.partial(_combine_kernel, tm=tm, final_norm=final_norm), grid=(nt,),
        in_specs=[pl.BlockSpec((1, 1, 2 * tm), lambda i: (i, 0, 0)),
                  pl.BlockSpec((1, 1, 2 * tm), lambda i: (jnp.minimum(i + 1, nt - 1), 0, 0)),
                  pl.BlockSpec(memory_space=pl.ANY),
                  pl.BlockSpec((tm, d), lambda i: (i, 0)),
                  pl.BlockSpec((1, 1, d), mod_map),
                  pl.BlockSpec((tm, LANES), lambda i: (i, 0)),
                  pl.BlockSpec((1, d), lambda i: (0, 0))],
        out_specs=pl.BlockSpec((tm, d), lambda i: (i, 0)),
        out_shape=jax.ShapeDtypeStruct((t, d), F32),
        scratch_shapes=[pltpu.VMEM((2, 2 * tm, d), F32),
                        pltpu.SMEM((2, 2 * tm), jnp.int32),
                        pltpu.SemaphoreType.DMA((2,)),
                        pltpu.SemaphoreType.DMA((1,))],
        compiler_params=_cparams(("arbitrary",)),
        name="combine")(idx, idx, ybuf, x, g2, rw, fn)


def _proj1_kernel(x_ref, nw_ref, sc_ref, sh_ref, w_ref, qn_ref, kn_ref, cos_ref, sin_ref,
                  *out_refs, rope, want_q):
    h = _norm_mod(x_ref[0], nw_ref[...], sc_ref[0], sh_ref[0]).astype(BF16)
    hd = ATTN_HEAD_DIM
    nq = ATTN_HEADS * hd
    nkv = ATTN_KV_HEADS * hd

    def head_norm(xh, gain):
        ms = jnp.mean(xh * xh, axis=-1, keepdims=True)
        xh = (xh * lax.rsqrt(ms + RMS_EPS)) * gain
        if rope:
            xh = xh * cos_ref[...] + pltpu.roll(xh, hd // 2, 1) * sin_ref[...]
        return xh

    if want_q:
        q_ref, k_ref, v_ref = out_refs
        qf = jnp.dot(h, w_ref[:, 0:nq], preferred_element_type=F32)
        for i in range(ATTN_HEADS):
            xh = head_norm(qf[:, i * hd:(i + 1) * hd], qn_ref[...])
            q_ref[0, :, i * hd:(i + 1) * hd] = (xh * (hd ** -0.5)).astype(BF16)
    else:
        k_ref, v_ref = out_refs
    kf = jnp.dot(h, w_ref[:, nq:nq + nkv], preferred_element_type=F32)
    for i in range(ATTN_KV_HEADS):
        k_ref[0, :, i * hd:(i + 1) * hd] = head_norm(kf[:, i * hd:(i + 1) * hd], kn_ref[...]).astype(BF16)
    v_ref[0] = jnp.dot(h, w_ref[:, nq + nkv:], preferred_element_type=F32).astype(BF16)


def _proj1(x, nw, sc, sh, w, qn, kn, cos_t, sin_t, tm, rope, want_q):
    b, l, d = x.shape
    per_batch = sc.shape[0] > 1
    mod_map = (lambda i, j: (i, 0, 0)) if per_batch else (lambda i, j: (0, 0, 0))
    row = lambda i, j: (i, j, 0)
    const2 = lambda i, j: (0, 0)
    hd = ATTN_HEAD_DIM
    nq, nkv = ATTN_HEADS * hd, ATTN_KV_HEADS * hd
    out_specs = [pl.BlockSpec((1, tm, nkv), row), pl.BlockSpec((1, tm, nkv), row)]
    out_shape = [jax.ShapeDtypeStruct((b, l, nkv), BF16), jax.ShapeDtypeStruct((b, l, nkv), BF16)]
    if want_q:
        out_specs = [pl.BlockSpec((1, tm, nq), row)] + out_specs
        out_shape = [jax.ShapeDtypeStruct((b, l, nq), BF16)] + out_shape
    return pl.pallas_call(
        functools.partial(_proj1_kernel, rope=rope, want_q=want_q), grid=(b, l // tm),
        in_specs=[pl.BlockSpec((1, tm, d), row), pl.BlockSpec((1, d), const2),
                  pl.BlockSpec((1, 1, d), mod_map), pl.BlockSpec((1, 1, d), mod_map),
                  pl.BlockSpec(w.shape, const2), pl.BlockSpec((1, hd), const2),
                  pl.BlockSpec((1, hd), const2),
                  pl.BlockSpec((tm, hd), lambda i, j: (j, 0)), pl.BlockSpec((tm, hd), lambda i, j: (j, 0))],
        out_specs=out_specs, out_shape=out_shape,
        compiler_params=_cparams(("arbitrary", "arbitrary")),
        name="proj1")(x, nw, sc, sh, w, qn, kn, cos_t, sin_t)


def _attn_kernel(q_ref, kl_ref, vl_ref, kc_ref, vc_ref, o_ref):
    hd = ATTN_HEAD_DIM
    k_l, v_l, k_c, v_c = kl_ref[0], vl_ref[0], kc_ref[0], vc_ref[0]
    nt = (((1,), (1,)), ((), ()))
    for g in range(ATTN_GROUP):
        qh = q_ref[0, :, g * hd:(g + 1) * hd]
        s_l = lax.dot_general(qh, k_l, nt, preferred_element_type=F32)
        s_c = lax.dot_general(qh, k_c, nt, preferred_element_type=F32)
        m = jnp.maximum(jnp.max(s_l, axis=-1, keepdims=True), jnp.max(s_c, axis=-1, keepdims=True))
        p_l = jnp.exp(s_l - m)
        p_c = jnp.exp(s_c - m)
        den = jnp.sum(p_l, axis=-1, keepdims=True) + jnp.sum(p_c, axis=-1, keepdims=True)
        o = (jnp.dot(p_l.astype(BF16), v_l, preferred_element_type=F32)
             + jnp.dot(p_c.astype(BF16), v_c, preferred_element_type=F32))
        o_ref[0, :, g * hd:(g + 1) * hd] = (o / den).astype(BF16)


def _attention(q, k_l, v_l, k_c, v_c, tq):
    b, l, nq = q.shape
    n_ctx = k_c.shape[1]
    hd = ATTN_HEAD_DIM
    gw = ATTN_GROUP * hd
    return pl.pallas_call(
        _attn_kernel, grid=(b, ATTN_KV_HEADS, l // tq),
        in_specs=[pl.BlockSpec((1, tq, gw), lambda i, kv, j: (i, j, kv)),
                  pl.BlockSpec((1, l, hd), lambda i, kv, j: (i, 0, kv)),
                  pl.BlockSpec((1, l, hd), lambda i, kv, j: (i, 0, kv)),
                  pl.BlockSpec((1, n_ctx, hd), lambda i, kv, j: (i, 0, kv)),
                  pl.BlockSpec((1, n_ctx, hd), lambda i, kv, j: (i, 0, kv))],
        out_specs=pl.BlockSpec((1, tq, gw), lambda i, kv, j: (i, j, kv)),
        out_shape=jax.ShapeDtypeStruct((b, l, nq), BF16),
        compiler_params=_cparams(("arbitrary", "arbitrary", "arbitrary")),
        name="attention")(q, k_l, v_l, k_c, v_c)


def _row_tile(n, pref):
    return pref if n % pref == 0 else n


def _moe(h2, re, rw_unused, w_gate, w_up, w_down):
    t = h2.shape[0]
    src, slot_of, blk_expert, blk_valid = _dispatch(re, t)
    ybuf = _experts(h2, src, blk_expert, blk_valid, w_gate, w_up, w_down)
    return ybuf, slot_of


def kernel(x, c, ctx, c_ctx, mod_w, mod_b, norm1, norm2, ab_w_in, ab_conv_a, ab_conv_qkv, ab_a_log,
           ab_dt_bias, ab_gnorm, ab_w_out, attn_w_qkv, attn_q_norm, attn_k_norm, attn_w_o, moe_w_group,
           moe_w_expert, moe_w_gate, moe_w_up, moe_w_down, final_norm):
    b, l, d = x.shape
    n_ctx = ctx.shape[1]
    assert mod_w.shape[0] == 2 and d == D_MODEL
    assert l % PRE_TILE == 0 and n_ctx % SUPER == 0 and n_ctx % 16 == 0
    tm_l = _row_tile(l, 512)
    tm_c = _row_tile(n_ctx, 256)

    rpad = (-(b + 1)) % 8
    cond = jnp.concatenate([c, c_ctx[None, :], jnp.zeros((rpad, d), F32)], axis=0)
    mod = _modulation(cond, mod_w, mod_b)

    def mods(i):
        lat = [mod[i, :b, k * d:(k + 1) * d].reshape(b, 1, d) for k in range(6)]
        cx = [mod[i, b:b + 1, k * d:(k + 1) * d].reshape(1, 1, d) for k in range(6)]
        return lat, cx

    def router_w(i):
        w = jnp.concatenate([moe_w_group[i], moe_w_expert[i]], axis=1)
        return jnp.pad(w, ((0, 0), (0, LANES - w.shape[1])))

    (sh1, sc1, g1, sh2, sc2, g2), (csh1, csc1, cg1, csh2, csc2, cg2) = mods(0)
    w_in = ab_w_in[0]
    n_gate = 4 * GDN_HEADS
    w0 = jnp.concatenate([w_in[:, :P0_A + QKV_WIDTH + P0_Z], w_in[:, -n_gate:],
                          jnp.zeros((d, LANES - n_gate), F32)], axis=1).astype(BF16)
    nw1 = norm1[0].reshape(1, d)
    a_l, qkv_l, z_l, gt_l = _proj0(x, nw1, sc1, sh1, w0, tm_l)
    a_c, qkv_c, z_c, gt_c = _proj0(ctx, nw1, csc1, csh1, w0, tm_c)

    def pair_gates(gt):
        n = gt.shape[1]
        g4 = gt[:, :, :n_gate].reshape(b, n, 4, GDN_PAIRS, 2)
        return jnp.transpose(g4, (0, 3, 1, 2, 4)).reshape(b, GDN_PAIRS, n, 8)

    def pair_vec(v):
        v2 = v.reshape(2, GDN_PAIRS, 2)
        z2 = jnp.zeros_like(v2)
        return jnp.stack([v2[0], z2[0], v2[1], z2[1]], axis=1).reshape(GDN_PAIRS, 1, 8)

    gnorm2 = jnp.tile(ab_gnorm[0], 2).reshape(1, LANES)
    ya_c, yb_c, ya_l, yb_l = _gdn_mixer(
        a_c, qkv_c, z_c, pair_gates(gt_c), a_l, qkv_l, z_l, pair_gates(gt_l),
        ab_conv_qkv[0], ab_conv_a[0], pair_vec(ab_a_log[0]), pair_vec(ab_dt_bias[0]), gnorm2)

    w_out = ab_w_out[0].astype(BF16)
    w_out_a, w_out_b = w_out[:A_WIDTH], w_out[A_WIDTH:]
    nw2 = norm2[0].reshape(1, d)
    wr0 = router_w(0)
    x1, h2_l, re_l, rw_l = _mix_out([ya_l, yb_l], [w_out_a, w_out_b], x, g1, nw2, sc2, sh2, wr0, tm_l)
    c1, h2_c, re_c, rw_c = _mix_out([ya_c, yb_c], [w_out_a, w_out_b], ctx, cg1, nw2, csc2, csh2, wr0, tm_c)

    t_l, t_c = b * l, b * n_ctx
    h2 = jnp.concatenate([h2_l.reshape(t_l, d), h2_c.reshape(t_c, d)], axis=0)
    re = jnp.concatenate([re_l.reshape(t_l, LANES)[:, :2], re_c.reshape(t_c, LANES)[:, :2]], axis=0)
    ybuf, slot_of = _moe(h2, re, None, moe_w_gate[0], moe_w_up[0], moe_w_down[0])
    fn = final_norm.reshape(1, d)
    x2 = _combine(ybuf, slot_of[:t_l], x1.reshape(t_l, d), g2, rw_l.reshape(t_l, LANES), fn,
                  l, _row_tile(l, 256), False).reshape(b, l, d)
    ctx2 = _combine(ybuf, slot_of[t_l:], c1.reshape(t_c, d), cg2, rw_c.reshape(t_c, LANES), fn,
                    n_ctx, _row_tile(n_ctx, 256), False).reshape(b, n_ctx, d)

    (sh1, sc1, g1, sh2, sc2, g2), (csh1, csc1, cg1, csh2, csc2, cg2) = mods(1)
    hd = ATTN_HEAD_DIM
    perm = jnp.concatenate([jnp.arange(0, hd, 2), jnp.arange(1, hd, 2)])
    n_qk = (ATTN_HEADS + ATTN_KV_HEADS) * hd
    col_perm = (jnp.arange(n_qk).reshape(-1, hd)[:, perm]).reshape(-1)
    w_qkv = attn_w_qkv[0]
    w1 = jnp.concatenate([w_qkv[:, col_perm], w_qkv[:, n_qk:]], axis=1).astype(BF16)
    qn = attn_q_norm[0][perm].reshape(1, hd)
    kn = attn_k_norm[0][perm].reshape(1, hd)
    pos = jnp.arange(l, dtype=jnp.int32)
    n_freq = hd // 4
    inv = ROPE_THETA ** (-jnp.arange(n_freq, dtype=F32) / n_freq)
    ang = jnp.concatenate([(pos // GRID_W).astype(F32)[:, None] * inv,
                           (pos % GRID_W).astype(F32)[:, None] * inv], axis=-1)
    cos_t = jnp.concatenate([jnp.cos(ang), jnp.cos(ang)], axis=-1)
    sin_t = jnp.concatenate([-jnp.sin(ang), jnp.sin(ang)], axis=-1)
    nw1 = norm1[1].reshape(1, d)
    q_l, k_l, v_l = _proj1(x2, nw1, sc1, sh1, w1, qn, kn, cos_t, sin_t, tm_l, True, True)
    dummy = jnp.zeros((n_ctx, hd), F32)
    k_c, v_c = _proj1(ctx2, nw1, csc1, csh1, w1, qn, kn, dummy, dummy, tm_c, False, False)
    o = _attention(q_l, k_l, v_l, k_c, v_c, _row_tile(l, 256))

    nw2 = norm2[1].reshape(1, d)
    x3, h2_l, re_l, rw_l = _mix_out([o], [attn_w_o[0].astype(BF16)], x2, g1, nw2, sc2, sh2,
                                    router_w(1), tm_l)
    ybuf, slot_of = _moe(h2_l.reshape(t_l, d), re_l.reshape(t_l, LANES)[:, :2], None,
                         moe_w_gate[1], moe_w_up[1], moe_w_down[1])
    out = _combine(ybuf, slot_of, x3.reshape(t_l, d), g2, rw_l.reshape(t_l, LANES), fn,
                   l, _row_tile(l, 256), True)
    return out.reshape(b, l, d)
```

```python
import functools

import jax
import jax.numpy as jnp
from jax import lax
from jax.experimental import pallas as pl
from jax.experimental.pallas import tpu as pltpu

F32 = jnp.float32
BF16 = jnp.bfloat16
HI = lax.Precision.HIGHEST

D_MODEL = 1024
RMS_EPS = 1e-6
L2_EPS = 1e-6
GRID_W = 64
A_WIDTH = 512
GDN_HEADS = 8
GDN_DK = 64
GDN_DV = 64
GDN_CHUNK = 64
QKV_WIDTH = 1536
ATTN_HEADS = 8
ATTN_KV_HEADS = 2
ATTN_GROUP = 4
ATTN_HEAD_DIM = 128
ROPE_THETA = 10000.0
N_GROUPS = 8
EXPERTS_PER_GROUP = 8
N_EXPERTS = 64
D_EXPERT = 384
MOE_BLOCK = 256

LANES = 128
GDN_PAIRS = GDN_HEADS // 2
SUPER = 2 * GDN_CHUNK
VMEM_LIMIT = 52 * 1024 * 1024


def _cparams(sem):
    return pltpu.CompilerParams(dimension_semantics=sem, vmem_limit_bytes=VMEM_LIMIT)


def _silu(x):
    return x * jax.nn.sigmoid(x)


def _norm_mod(x, nw, sc, sh):
    ms = jnp.mean(x * x, axis=-1, keepdims=True)
    return (x * lax.rsqrt(ms + RMS_EPS)) * nw * (1.0 + sc) + sh


def _bdot(a, b):
    return jnp.dot(a.astype(BF16), b.astype(BF16), preferred_element_type=F32)


def _bdot_nt(a, b):
    return lax.dot_general(a.astype(BF16), b.astype(BF16), (((1,), (1,)), ((), ())),
                           preferred_element_type=F32)


def _bdot_tn(a, b):
    return lax.dot_general(a.astype(BF16), b.astype(BF16), (((0,), (0,)), ((), ())),
                           preferred_element_type=F32)


def _mod_kernel(c_ref, w_ref, b_ref, o_ref):
    c = c_ref[...]
    o_ref[0] = jnp.dot(_silu(c), w_ref[0], precision=HI, preferred_element_type=F32) + b_ref[0]


def _modulation(cond, mod_w, mod_b):
    depth, d, n = mod_w.shape
    r = cond.shape[0]
    tn = 512
    return pl.pallas_call(
        _mod_kernel, grid=(depth, n // tn),
        in_specs=[pl.BlockSpec((r, d), lambda i, j: (0, 0)),
                  pl.BlockSpec((1, d, tn), lambda i, j: (i, 0, j)),
                  pl.BlockSpec((1, 1, tn), lambda i, j: (i, 0, j))],
        out_specs=pl.BlockSpec((1, r, tn), lambda i, j: (i, 0, j)),
        out_shape=jax.ShapeDtypeStruct((depth, r, n), F32),
        compiler_params=_cparams(("arbitrary", "arbitrary")),
        name="modulation")(cond, mod_w, mod_b.reshape(depth, 1, n))


P0_A = 3 * A_WIDTH
P0_Z = GDN_HEADS * GDN_DV
P0_COLS = P0_A + QKV_WIDTH + P0_Z + LANES


def _proj0_kernel(x_ref, nw_ref, sc_ref, sh_ref, w_ref, a_ref, qkv_ref, z_ref, g_ref):
    h = _norm_mod(x_ref[0], nw_ref[...], sc_ref[0], sh_ref[0]).astype(BF16)
    c0, c1, c2 = P0_A, P0_A + QKV_WIDTH, P0_A + QKV_WIDTH + P0_Z
    a_ref[0] = jnp.dot(h, w_ref[:, 0:c0], preferred_element_type=F32).astype(BF16)
    qkv_ref[0] = jnp.dot(h, w_ref[:, c0:c1], preferred_element_type=F32).astype(BF16)
    z_ref[0] = jnp.dot(h, w_ref[:, c1:c2], preferred_element_type=F32).astype(BF16)
    g_ref[0] = jnp.dot(h, w_ref[:, c2:], preferred_element_type=F32)


def _proj0(x, nw, sc, sh, w, tm):
    b, l, d = x.shape
    per_batch = sc.shape[0] > 1
    mod_map = (lambda i, j: (i, 0, 0)) if per_batch else (lambda i, j: (0, 0, 0))
    row = lambda i, j: (i, j, 0)
    return pl.pallas_call(
        _proj0_kernel, grid=(b, l // tm),
        in_specs=[pl.BlockSpec((1, tm, d), row),
                  pl.BlockSpec((1, d), lambda i, j: (0, 0)),
                  pl.BlockSpec((1, 1, d), mod_map),
                  pl.BlockSpec((1, 1, d), mod_map),
                  pl.BlockSpec((d, P0_COLS), lambda i, j: (0, 0))],
        out_specs=[pl.BlockSpec((1, tm, P0_A), row),
                   pl.BlockSpec((1, tm, QKV_WIDTH), row),
                   pl.BlockSpec((1, tm, P0_Z), row),
                   pl.BlockSpec((1, tm, LANES), row)],
        out_shape=[jax.ShapeDtypeStruct((b, l, P0_A), BF16),
                   jax.ShapeDtypeStruct((b, l, QKV_WIDTH), BF16),
                   jax.ShapeDtypeStruct((b, l, P0_Z), BF16),
                   jax.ShapeDtypeStruct((b, l, LANES), F32)],
        compiler_params=_cparams(("arbitrary", "arbitrary")),
        name="proj0")(x, nw, sc, sh, w)


PRE_TILE = 256
P1_UNITS = 2


def _gdn_kernel(qc_ref, kc_ref, vc_ref, ql_ref, kl_ref, vl_ref,
                xac_ref, gbc_ref, gcc_ref, xal_ref, gbl_ref, gcl_ref,
                zc_ref, zl_ref, gtc_ref, gtl_ref,
                cwq_ref, cwk_ref, cwv_ref, cwa_ref, a8_ref, dt8_ref, gn_ref,
                yac_ref, ybc_ref, yal_ref, ybl_ref,
                qn_s, kn_s, vn_s, g_s, bt_s, m_s, qq_s, qe_s, ou_s, gl_s, o_s,
                *, n_ctx, n_lat):
    n_tot = n_ctx + n_lat
    lane = lax.broadcasted_iota(jnp.int32, (1, LANES), 1)
    lo = lane < GDN_DK
    lo2 = (lax.broadcasted_iota(jnp.int32, (1, 2 * LANES), 1) & (LANES - 1)) < GDN_DK
    head_blk = (lax.broadcasted_iota(jnp.int32, (LANES, LANES), 0) // GDN_DK
                == lax.broadcasted_iota(jnp.int32, (LANES, LANES), 1) // GDN_DK)
    blk_ones = jnp.where(head_blk, 1.0, 0.0).astype(F32)

    def conv3_tile(ref, r0, n, w_ref, mul_ref=None):
        def rows(start, size):
            v = ref[0, pl.ds(start, size), :].astype(F32)
            if mul_ref is not None:
                v = v * mul_ref[0, pl.ds(start, size), :].astype(F32)
            return v
        cur = rows(r0, PRE_TILE)
        prev = rows(pl.multiple_of(jnp.maximum(r0 - 16, 0), 16), 16)[15:16, :] * (r0 > 0).astype(F32)
        nxt = (rows(pl.multiple_of(jnp.minimum(r0 + PRE_TILE, n - 16), 16), 16)[0:1, :]
               * (r0 + PRE_TILE < n).astype(F32))
        row = lax.broadcasted_iota(jnp.int32, (PRE_TILE, 1), 0)
        xm = jnp.where(row == 0, prev, pltpu.roll(cur, 1, 0))
        xp = jnp.where(row == PRE_TILE - 1, nxt, pltpu.roll(cur, PRE_TILE - 1, 0))
        return xm * w_ref[0:1, :] + cur * w_ref[1:2, :] + xp * w_ref[2:3, :]

    def head_sumsq(x):
        return jnp.dot(x * x, blk_ones, precision=HI, preferred_element_type=F32)

    def prepass(seg_off, n, q_ref, k_ref, v_ref, gt_ref):
        def body(t, carry):
            r0 = pl.multiple_of(t * PRE_TILE, PRE_TILE)
            dst = pl.ds(pl.multiple_of(seg_off + r0, PRE_TILE), PRE_TILE)
            q = _silu(conv3_tile(q_ref, r0, n, cwq_ref))
            k = _silu(conv3_tile(k_ref, r0, n, cwk_ref))
            v = _silu(conv3_tile(v_ref, r0, n, cwv_ref))
            qn_s[dst, :] = q * lax.rsqrt(head_sumsq(q) + L2_EPS) * (GDN_DK ** -0.5)
            kn_s[dst, :] = k * lax.rsqrt(head_sumsq(k) + L2_EPS)
            vn_s[dst, :] = v
            gt = gt_ref[0, 0, pl.ds(r0, PRE_TILE), :]
            sp_in = gt + dt8_ref[0]
            softplus = jnp.maximum(sp_in, 0.0) + jnp.log1p(jnp.exp(-jnp.abs(sp_in)))
            graw = -jnp.exp(a8_ref[0]) * softplus
            beta = jax.nn.sigmoid(gt)
            for d in range(2):
                c = 4 * d
                g_s[d, dst, :] = jnp.where(lo, graw[:, c:c + 1], graw[:, c + 1:c + 2])
                bt_s[d, dst, :] = jnp.where(lo, beta[:, c + 2:c + 3], beta[:, c + 3:c + 4])
            return carry
        lax.fori_loop(0, n // PRE_TILE, body, 0)

    prepass(0, n_ctx, qc_ref, kc_ref, vc_ref, gtc_ref)
    prepass(n_ctx, n_lat, ql_ref, kl_ref, vl_ref, gtl_ref)

    ii = lax.broadcasted_iota(jnp.int32, (SUPER, SUPER), 0)
    jj = lax.broadcasted_iota(jnp.int32, (SUPER, SUPER), 1)
    xr = ii ^ jj
    same_chunk = xr < GDN_CHUNK
    top = lax.broadcasted_iota(jnp.int32, (SUPER, 1), 0) < GDN_CHUNK
    eye = jnp.where(ii == jj, 1.0, 0.0).astype(F32)
    cs_all = jnp.where(same_chunk, 1.0, 0.0).astype(F32)

    chains = ((0, 0), (0, 1), (1, 0), (1, 1))

    zero_b = jnp.zeros((SUPER, SUPER), BF16)

    def setup_unit(r0):
        rows = pl.ds(r0, SUPER)
        q_s = qn_s[rows, :]
        k_s = kn_s[rows, :]
        v_s = vn_s[rows, :]
        kb = k_s.astype(BF16)
        gram, qk = [], []
        for hh in range(2):
            hm = lo if hh == 0 else jnp.logical_not(lo)
            gram.append(_bdot_nt(jnp.where(hm, k_s, 0.0), kb))
            qk.append(_bdot_nt(jnp.where(hm, q_s, 0.0), kb))
        valid, b_l, gc, gcl = [], [], [], []
        for d in range(2):
            tri = (jj <= ii) if d == 0 else (jj >= ii)
            valid.append(jnp.logical_and(same_chunk, tri))
            g_l = g_s[d, rows, :]
            b_l.append(bt_s[d, rows, :])
            gc.append(jnp.dot(jnp.where(valid[d], 1.0, 0.0).astype(F32), g_l, precision=HI,
                              preferred_element_type=F32))
            gcl.append(jnp.dot(cs_all, g_l, precision=HI, preferred_element_type=F32))
        u = dict(r0=r0, rows=rows, qg=[], kg=[], rhs=[], nm=[], att=[], t_inv=[])
        gct = []
        for d in range(2):
            egc = jnp.exp(gc[d])
            u['qg'].append(q_s * egc)
            u['kg'].append((k_s * jnp.exp(gcl[d] - gc[d])).astype(BF16))
            gl_s[d, rows, :] = jnp.exp(gcl[d])
            u['rhs'].append(jnp.concatenate([v_s * b_l[d], k_s * b_l[d] * egc], axis=1).astype(BF16))
            gct.append(gc[d].T)
        for d, hh in chains:
            c0 = hh * GDN_DK
            diff = gc[d][:, c0:c0 + 1] - gct[d][c0:c0 + 1, :]
            dec = jnp.exp(jnp.where(valid[d], diff, -jnp.inf))
            u['nm'].append(jnp.where(ii == jj, 0.0, b_l[d][:, c0:c0 + 1] * gram[hh] * dec).astype(BF16))
            u['att'].append((qk[hh] * dec).astype(BF16))
            u['t_inv'].append(eye - jnp.where(xr < 2, u['nm'][-1].astype(F32), 0.0))
        return u

    def finish_unit(u, uw, auw):
        for d in range(2):
            c_a, c_b = 2 * d, 2 * d + 1
            uw_d = jnp.where(lo2, uw[c_a], uw[c_b])
            auw_d = jnp.where(lo2, auw[c_a], auw[c_b])
            ou_s[d, u['rows'], :] = auw_d[:, :LANES]
            qe_s[d, u['rows'], :] = (u['qg'][d] - auw_d[:, LANES:]).astype(BF16)
            uw_b = uw_d.astype(BF16)
            kg = u['kg'][d]
            for half in range(2):
                keep = top if half == 0 else jnp.logical_not(top)
                mq = _bdot_tn(jnp.where(keep, kg, jnp.zeros_like(kg)), uw_b)
                crow = pl.ds(pl.multiple_of(2 * u['r0'] + half * LANES, LANES), LANES)
                qq_s[d, crow, :] = jnp.where(head_blk, mq[:, :LANES], 0.0)
                m_s[d, crow, :] = jnp.where(head_blk, mq[:, LANES:], 0.0).astype(BF16)

    def phase1(s, carry):
        units = [setup_unit(pl.multiple_of((s * P1_UNITS + k) * SUPER, SUPER)) for k in range(P1_UNITS)]
        nm = [m for u in units for m in u['nm']]
        t_inv = [t for u in units for t in u['t_inv']]
        nc = len(nm)
        for lvl in range(1, 6):
            sel = (xr >> lvl) == 1
            tb = [t.astype(BF16) for t in t_inv]
            ot = [jnp.dot(jnp.where(sel, nm[c], zero_b), tb[c], preferred_element_type=F32) for c in range(nc)]
            tot = [jnp.dot(tb[c], ot[c].astype(BF16), preferred_element_type=F32) for c in range(nc)]
            t_inv = [t_inv[c] - tot[c] for c in range(nc)]
        rhs = [u['rhs'][d] for u in units for d, _ in chains]
        att = [a for u in units for a in u['att']]
        uw = [jnp.dot(t_inv[c].astype(BF16), rhs[c], preferred_element_type=F32) for c in range(nc)]
        auw = [jnp.dot(att[c], uw[c].astype(BF16), preferred_element_type=F32) for c in range(nc)]
        for k, u in enumerate(units):
            finish_unit(u, uw[4 * k:4 * k + 4], auw[4 * k:4 * k + 4])
        return carry

    lax.fori_loop(0, n_tot // (SUPER * P1_UNITS), phase1, 0)

    n_ch = n_tot // GDN_CHUNK
    n_cc = n_ctx // GDN_CHUNK

    def scan_step(n, states):
        c_f = n
        c_b = jnp.where(n < n_cc, n_cc - 1 - n, n_ch - 1 + n_cc - n)
        new_states = []
        for d, c in ((0, c_f), (1, c_b)):
            st = states[d]
            rows = pl.ds(pl.multiple_of(c * GDN_CHUNK, GDN_CHUNK), GDN_CHUNK)
            crow = pl.ds(pl.multiple_of(c * LANES, LANES), LANES)
            lhs = jnp.concatenate([m_s[d, crow, :], qe_s[d, rows, :]], axis=0)
            res = jnp.dot(lhs, st.astype(BF16), preferred_element_type=F32)
            o_s[d, rows, :] = res[LANES:] + ou_s[d, rows, :]
            gl = gl_s[d, pl.ds(pl.multiple_of(c * GDN_CHUNK, GDN_CHUNK), 1), :]
            new_states.append(st * gl + qq_s[d, crow, :] - res[:LANES])
        return tuple(new_states)

    s0 = jnp.zeros((LANES, LANES), F32)
    lax.fori_loop(0, n_ch, scan_step, (s0, s0))

    def postpass(seg_off, n, z_ref, xa_ref, gb_ref, gcv_ref, ya_ref, yb_ref):
        def body(t, carry):
            r0 = pl.multiple_of(t * PRE_TILE, PRE_TILE)
            src = pl.ds(pl.multiple_of(seg_off + r0, PRE_TILE), PRE_TILE)
            dst = pl.ds(r0, PRE_TILE)
            o = o_s[0, src, :] + o_s[1, src, :]
            ms = head_sumsq(o) * (1.0 / GDN_DV)
            z = z_ref[0, dst, :].astype(F32)
            yb_ref[0, dst, :] = ((o * lax.rsqrt(ms + RMS_EPS)) * gn_ref[...] * _silu(z)).astype(BF16)
            conv = conv3_tile(gcv_ref, r0, n, cwa_ref, mul_ref=xa_ref)
            ya_ref[0, dst, :] = (gb_ref[0, dst, :].astype(F32) * conv).astype(BF16)
            return carry
        lax.fori_loop(0, n // PRE_TILE, body, 0)

    postpass(0, n_ctx, zc_ref, xac_ref, gbc_ref, gcc_ref, yac_ref, ybc_ref)
    postpass(n_ctx, n_lat, zl_ref, xal_ref, gbl_ref, gcl_ref, yal_ref, ybl_ref)


def _gdn_mixer(a_c, qkv_c, z_c, gt_c, a_l, qkv_l, z_l, gt_l, conv_qkv, conv_a, a8, dt8, gnorm2):
    b, n_ctx, _ = qkv_c.shape
    n_lat = qkv_l.shape[1]
    n_tot = n_ctx + n_lat
    np_ = GDN_PAIRS

    def seq(n, off):
        return pl.BlockSpec((1, n, LANES), lambda i, p, off=off: (i, 0, off + p))

    def cw(off):
        return pl.BlockSpec((3, LANES), lambda i, p, off=off: (0, off + p))

    gate = lambda n: pl.BlockSpec((1, 1, n, 8), lambda i, p: (i, p, 0, 0))
    vec8 = pl.BlockSpec((1, 1, 8), lambda i, p: (p, 0, 0))
    in_specs = [seq(n_ctx, 0), seq(n_ctx, np_), seq(n_ctx, 2 * np_),
                seq(n_lat, 0), seq(n_lat, np_), seq(n_lat, 2 * np_),
                seq(n_ctx, 0), seq(n_ctx, np_), seq(n_ctx, 2 * np_),
                seq(n_lat, 0), seq(n_lat, np_), seq(n_lat, 2 * np_),
                seq(n_ctx, 0), seq(n_lat, 0), gate(n_ctx), gate(n_lat),
                cw(0), cw(np_), cw(2 * np_), cw(0), vec8, vec8,
                pl.BlockSpec((1, LANES), lambda i, p: (0, 0))]
    out_specs = [seq(n_ctx, 0), seq(n_ctx, 0), seq(n_lat, 0), seq(n_lat, 0)]
    out_shape = [jax.ShapeDtypeStruct((b, n_ctx, A_WIDTH), BF16),
                 jax.ShapeDtypeStruct((b, n_ctx, GDN_HEADS * GDN_DV), BF16),
                 jax.ShapeDtypeStruct((b, n_lat, A_WIDTH), BF16),
                 jax.ShapeDtypeStruct((b, n_lat, GDN_HEADS * GDN_DV), BF16)]
    f32buf = lambda *lead: pltpu.VMEM((*lead, n_tot, LANES), F32)
    scratch = [f32buf(), f32buf(), f32buf(), f32buf(2), f32buf(2),
               pltpu.VMEM((2, 2 * n_tot, LANES), BF16), pltpu.VMEM((2, 2 * n_tot, LANES), F32),
               pltpu.VMEM((2, n_tot, LANES), BF16), f32buf(2), f32buf(2), f32buf(2)]
    return pl.pallas_call(
        functools.partial(_gdn_kernel, n_ctx=n_ctx, n_lat=n_lat),
        grid=(b, np_), in_specs=in_specs, out_specs=out_specs, out_shape=out_shape,
        scratch_shapes=scratch,
        compiler_params=_cparams(("arbitrary", "arbitrary")),
        name="gdn_mixer")(qkv_c, qkv_c, qkv_c, qkv_l, qkv_l, qkv_l,
                          a_c, a_c, a_c, a_l, a_l, a_l, z_c, z_l, gt_c, gt_l,
                          conv_qkv, conv_qkv, conv_qkv, conv_a, a8, dt8, gnorm2)


def _route(logits):
    lane = lax.broadcasted_iota(jnp.int32, logits.shape, 1)
    lane_f = lane.astype(F32)
    neg = -jnp.inf
    is_g = lane < N_GROUPS
    gl = jnp.where(is_g, logits, neg)
    gmax = jnp.max(gl, axis=-1, keepdims=True)
    grp = jnp.min(jnp.where(gl == gmax, lane_f, float(LANES)), axis=-1, keepdims=True)
    gsum = jnp.sum(jnp.where(is_g, jnp.exp(gl - gmax), 0.0), axis=-1, keepdims=True)
    gprob = 1.0 / gsum
    first = float(N_GROUPS) + grp * float(EXPERTS_PER_GROUP)
    ing = jnp.logical_and(lane_f >= first, lane_f < first + float(EXPERTS_PER_GROUP))
    el = jnp.where(ing, logits, neg)
    v1 = jnp.max(el, axis=-1, keepdims=True)
    i1 = jnp.min(jnp.where(el == v1, lane_f, float(LANES)), axis=-1, keepdims=True)
    el2 = jnp.where(lane_f == i1, neg, el)
    v2 = jnp.max(el2, axis=-1, keepdims=True)
    i2 = jnp.min(jnp.where(el2 == v2, lane_f, float(LANES)), axis=-1, keepdims=True)
    t = jnp.exp(v2 - v1)
    den = 1.0 + t
    w1 = (1.0 / den) * gprob
    w2 = (t / den) * gprob
    e1 = i1 - float(N_GROUPS)
    e2 = i2 - float(N_GROUPS)
    e_out = jnp.where(lane == 0, e1, jnp.where(lane == 1, e2, 0.0))
    w_out = jnp.where(lane == 0, w1, jnp.where(lane == 1, w2, 0.0))
    picked = jnp.where(jnp.logical_or(lane_f == e1, lane_f == e2), 1.0, 0.0)
    return e_out.astype(jnp.int32), w_out, jnp.sum(picked, axis=0, keepdims=True)


def _mix_out_kernel(*refs, n_parts):
    ys = refs[:n_parts]
    ws = refs[n_parts:2 * n_parts]
    x_ref, g1_ref, nw_ref, sc_ref, sh_ref, wr_ref = refs[2 * n_parts:2 * n_parts + 6]
    xo_ref, h2_ref, re_ref, rw_ref, cnt_ref = refs[2 * n_parts + 6:]
    y = jnp.dot(ys[0][0], ws[0][...], preferred_element_type=F32)
    for k in range(1, n_parts):
        y = y + jnp.dot(ys[k][0], ws[k][...], preferred_element_type=F32)
    x = x_ref[0] + g1_ref[0] * y
    xo_ref[0] = x
    h2 = _norm_mod(x, nw_ref[...], sc_ref[0], sh_ref[0])
    h2_ref[0] = h2
    logits = jnp.dot(h2, wr_ref[...], precision=HI, preferred_element_type=F32)
    e_out, w_out, cnt = _route(logits)
    re_ref[0] = e_out
    rw_ref[0] = w_out

    @pl.when(jnp.logical_and(pl.program_id(0) == 0, pl.program_id(1) == 0))
    def _():
        cnt_ref[...] = jnp.zeros_like(cnt_ref)

    cnt_ref[...] += cnt


def _mix_out(parts, weights, x, g1, nw, sc, sh, w_router, tm):
    b, l, d = x.shape
    n_parts = len(parts)
    per_batch = g1.shape[0] > 1
    mod_map = (lambda i, j: (i, 0, 0)) if per_batch else (lambda i, j: (0, 0, 0))
    row = lambda i, j: (i, j, 0)
    const2 = lambda i, j: (0, 0)
    in_specs = ([pl.BlockSpec((1, tm, p.shape[2]), row) for p in parts]
                + [pl.BlockSpec(w.shape, const2) for w in weights]
                + [pl.BlockSpec((1, tm, d), row), pl.BlockSpec((1, 1, d), mod_map),
                   pl.BlockSpec((1, d), const2), pl.BlockSpec((1, 1, d), mod_map),
                   pl.BlockSpec((1, 1, d), mod_map), pl.BlockSpec((d, LANES), const2)])
    return pl.pallas_call(
        functools.partial(_mix_out_kernel, n_parts=n_parts), grid=(b, l // tm),
        in_specs=in_specs,
        out_specs=[pl.BlockSpec((1, tm, d), row), pl.BlockSpec((1, tm, d), row),
                   pl.BlockSpec((1, tm, LANES), row), pl.BlockSpec((1, tm, LANES), row),
                   pl.BlockSpec((1, LANES), const2)],
        out_shape=[jax.ShapeDtypeStruct((b, l, d), F32), jax.ShapeDtypeStruct((b, l, d), F32),
                   jax.ShapeDtypeStruct((b, l, LANES), jnp.int32),
                   jax.ShapeDtypeStruct((b, l, LANES), F32),
                   jax.ShapeDtypeStruct((1, LANES), F32)],
        compiler_params=_cparams(("arbitrary", "arbitrary")),
        name="mix_out")(*parts, *weights, x, g1, nw, sc, sh, w_router)


def _block_layout(counts, n_blocks):
    counts = counts.astype(jnp.int32)
    padded = (counts + MOE_BLOCK - 1) // MOE_BLOCK * MOE_BLOCK
    pad_ends = jnp.cumsum(padded)
    pad_starts = pad_ends - padded
    blk_start = jnp.arange(n_blocks, dtype=jnp.int32) * MOE_BLOCK
    blk_expert = jnp.minimum(jnp.sum((blk_start[:, None] >= pad_ends[None, :]).astype(jnp.int32), axis=1),
                             N_EXPERTS - 1).astype(jnp.int32)
    blk_valid = (blk_start < pad_ends[-1]).astype(jnp.int32)
    return pad_starts, blk_expert, blk_valid


SLOT_TILE = 512


def _slot_kernel(re_ref, base_ref, slot_ref, carry):
    @pl.when(pl.program_id(0) == 0)
    def _():
        carry[...] = jnp.zeros_like(carry)

    e = re_ref[...]
    r = e.shape[0]
    lane = lax.broadcasted_iota(jnp.int32, e.shape, 1)
    oh0 = lane == e[:, 0:1]
    oh1 = lane == e[:, 1:2]
    both = jnp.where(jnp.logical_or(oh0, oh1), 1.0, 0.0)
    strict = (lax.broadcasted_iota(jnp.int32, (r, r), 0) > lax.broadcasted_iota(jnp.int32, (r, r), 1))
    before = jnp.dot(jnp.where(strict, 1.0, 0.0).astype(BF16), both.astype(BF16),
                     preferred_element_type=F32)
    pos = before + carry[...] + base_ref[...]
    s0 = jnp.sum(jnp.where(oh0, pos, 0.0), axis=-1, keepdims=True)
    s1 = jnp.sum(jnp.where(oh1, pos, 0.0), axis=-1, keepdims=True)
    slot_ref[...] = jnp.where(lane == 0, s0, jnp.where(lane == 1, s1, 0.0)).astype(jnp.int32)
    carry[...] += jnp.sum(both, axis=0, keepdims=True)


def _slots(re, base):
    t = re.shape[0]
    r = _row_tile(t, SLOT_TILE)
    return pl.pallas_call(
        _slot_kernel, grid=(t // r,),
        in_specs=[pl.BlockSpec((r, LANES), lambda i: (i, 0)), pl.BlockSpec((1, LANES), lambda i: (0, 0))],
        out_specs=pl.BlockSpec((r, LANES), lambda i: (i, 0)),
        out_shape=jax.ShapeDtypeStruct((t, LANES), jnp.int32),
        scratch_shapes=[pltpu.VMEM((1, LANES), F32)],
        compiler_params=_cparams(("arbitrary",)),
        name="slots")(re, base)


SCATTER_TILE = 256


def _scatter_kernel(idx_ref, h_hbm, xs_in, xs_out, idx_smem, ssem, isem, *, rt):
    del xs_in
    i = pl.program_id(0)
    n = pl.num_programs(0)
    slot = i % 2
    cp = pltpu.make_async_copy(idx_ref.at[0], idx_smem.at[pl.ds(slot, 1)], isem.at[0])
    cp.start()
    cp.wait()

    def issue(r, carry):
        src = h_hbm.at[pl.ds(i * rt + r, 1)]
        pltpu.make_async_copy(src, xs_out.at[pl.ds(idx_smem[slot, r], 1)], ssem.at[slot]).start()
        pltpu.make_async_copy(src, xs_out.at[pl.ds(idx_smem[slot, rt + r], 1)], ssem.at[slot]).start()
        return carry
    lax.fori_loop(0, rt, issue, 0, unroll=8)

    def drain(s):
        def body(r, carry):
            pltpu.make_async_copy(h_hbm.at[pl.ds(0, 1)], xs_out.at[pl.ds(0, 1)], ssem.at[s]).wait()
            return carry
        lax.fori_loop(0, 2 * rt, body, 0, unroll=8)

    @pl.when(i > 0)
    def _():
        drain(1 - slot)

    @pl.when(i == n - 1)
    def _():
        drain(slot)


def _scatter_rows(h2, slot_of, xs):
    t, d = h2.shape
    rt = _row_tile(t, SCATTER_TILE)
    nt = t // rt
    idx = jnp.concatenate([slot_of[:, 0].reshape(nt, 1, rt), slot_of[:, 1].reshape(nt, 1, rt)], axis=2)
    return pl.pallas_call(
        functools.partial(_scatter_kernel, rt=rt), grid=(nt,),
        in_specs=[pl.BlockSpec((1, 1, 2 * rt), lambda i: (i, 0, 0)),
                  pl.BlockSpec(memory_space=pl.ANY), pl.BlockSpec(memory_space=pl.ANY)],
        out_specs=pl.BlockSpec(memory_space=pl.ANY),
        out_shape=jax.ShapeDtypeStruct(xs.shape, xs.dtype),
        input_output_aliases={2: 0},
        scratch_shapes=[pltpu.SMEM((2, 2 * rt), jnp.int32),
                        pltpu.SemaphoreType.DMA((2,)), pltpu.SemaphoreType.DMA((1,))],
        compiler_params=_cparams(("arbitrary",)),
        name="scatter_rows")(idx, h2, xs)


def _gather_rows(idx_vmem_ref, idx_smem, slot, src_hbm, dst_buf, isem, gsem, n_rows):
    cp = pltpu.make_async_copy(idx_vmem_ref.at[0], idx_smem.at[pl.ds(slot, 1)], isem.at[0])
    cp.start()
    cp.wait()

    def body(r, carry):
        tok = idx_smem[slot, r]
        pltpu.make_async_copy(src_hbm.at[pl.ds(tok, 1)], dst_buf.at[slot, pl.ds(r, 1)],
                              gsem.at[slot]).start()
        return carry
    lax.fori_loop(0, n_rows, body, 0, unroll=8)


def _wait_rows(src_hbm, dst_buf, slot, gsem, n_rows):
    def body(r, carry):
        pltpu.make_async_copy(src_hbm.at[pl.ds(0, 1)], dst_buf.at[slot, pl.ds(r, 1)],
                              gsem.at[slot]).wait()
        return carry
    lax.fori_loop(0, n_rows, body, 0, unroll=8)


def _expert_kernel(be_ref, bv_ref, x_ref, wg_ref, wu_ref, wd_ref, out_ref):
    valid = bv_ref[pl.program_id(0)] > 0

    @pl.when(valid)
    def _():
        x = x_ref[...].astype(BF16)
        g = jnp.dot(x, wg_ref[0].astype(BF16), preferred_element_type=F32)
        u = jnp.dot(x, wu_ref[0].astype(BF16), preferred_element_type=F32)
        h = (_silu(g) * u).astype(BF16)
        out_ref[...] = jnp.dot(h, wd_ref[0].astype(BF16), preferred_element_type=F32)

    @pl.when(jnp.logical_not(valid))
    def _():
        out_ref[...] = jnp.zeros_like(out_ref)


def _experts(xs, blk_expert, blk_valid, w_gate, w_up, w_down):
    p, d = xs.shape
    n_blocks = p // MOE_BLOCK
    grid_spec = pltpu.PrefetchScalarGridSpec(
        num_scalar_prefetch=2, grid=(n_blocks,),
        in_specs=[pl.BlockSpec((MOE_BLOCK, d), lambda i, be, bv: (i, 0)),
                  pl.BlockSpec((1, d, D_EXPERT), lambda i, be, bv: (be[i], 0, 0)),
                  pl.BlockSpec((1, d, D_EXPERT), lambda i, be, bv: (be[i], 0, 0)),
                  pl.BlockSpec((1, D_EXPERT, d), lambda i, be, bv: (be[i], 0, 0))],
        out_specs=pl.BlockSpec((MOE_BLOCK, d), lambda i, be, bv: (i, 0)))
    return pl.pallas_call(
        _expert_kernel, grid_spec=grid_spec,
        out_shape=jax.ShapeDtypeStruct((p, d), F32),
        compiler_params=_cparams(("arbitrary",)),
        name="experts")(blk_expert, blk_valid, xs, w_gate, w_up, w_down)


def _combine_kernel(idx_ref, idx_nxt_ref, y_hbm, x_ref, g2_ref, rw_ref, fn_ref, out_ref,
                    rbuf, idx_smem, gsem, isem, *, tm, final_norm):
    i = pl.program_id(0)
    n = pl.num_programs(0)
    slot = i % 2

    @pl.when(i == 0)
    def _():
        _gather_rows(idx_ref, idx_smem, 0, y_hbm, rbuf, isem, gsem, 2 * tm)

    @pl.when(i + 1 < n)
    def _():
        _gather_rows(idx_nxt_ref, idx_smem, 1 - slot, y_hbm, rbuf, isem, gsem, 2 * tm)

    _wait_rows(y_hbm, rbuf, slot, gsem, 2 * tm)
    rw = rw_ref[...]
    y = rw[:, 0:1] * rbuf[slot, 0:tm, :] + rw[:, 1:2] * rbuf[slot, tm:2 * tm, :]
    x = x_ref[...] + g2_ref[0] * y
    if final_norm:
        ms = jnp.mean(x * x, axis=-1, keepdims=True)
        x = (x * lax.rsqrt(ms + RMS_EPS)) * fn_ref[...]
    out_ref[...] = x


def _combine(ybuf, slot_of, x, g2, rw, fn, rows_per_batch, tm, final_norm):
    t, d = x.shape
    nt = t // tm
    idx = jnp.concatenate([slot_of[:, 0].reshape(nt, 1, tm), slot_of[:, 1].reshape(nt, 1, tm)], axis=2)
    per_batch = g2.shape[0] > 1
    tpb = rows_per_batch // tm
    mod_map = (lambda i: (i // tpb, 0, 0)) if per_batch else (lambda i: (0, 0, 0))
    return pl.pallas_call(
        functools.partial(_combine_kernel, tm=tm, final_norm=final_norm), grid=(nt,),
        in_specs=[pl.BlockSpec((1, 1, 2 * tm), lambda i: (i, 0, 0)),
                  pl.BlockSpec((1, 1, 2 * tm), lambda i: (jnp.minimum(i + 1, nt - 1), 0, 0)),
                  pl.BlockSpec(memory_space=pl.ANY),
                  pl.BlockSpec((tm, d), lambda i: (i, 0)),
                  pl.BlockSpec((1, 1, d), mod_map),
                  pl.BlockSpec((tm, LANES), lambda i: (i, 0)),
                  pl.BlockSpec((1, d), lambda i: (0, 0))],
        out_specs=pl.BlockSpec((tm, d), lambda i: (i, 0)),
        out_shape=jax.ShapeDtypeStruct((t, d), F32),
        scratch_shapes=[pltpu.VMEM((2, 2 * tm, d), F32),
                        pltpu.SMEM((2, 2 * tm), jnp.int32),
                        pltpu.SemaphoreType.DMA((2,)),
                        pltpu.SemaphoreType.DMA((1,))],
        compiler_params=_cparams(("arbitrary",)),
        name="combine")(idx, idx, ybuf, x, g2, rw, fn)


def _proj1_kernel(x_ref, nw_ref, sc_ref, sh_ref, w_ref, qn_ref, kn_ref, cos_ref, sin_ref,
                  *out_refs, rope, want_q):
    h = _norm_mod(x_ref[0], nw_ref[...], sc_ref[0], sh_ref[0]).astype(BF16)
    hd = ATTN_HEAD_DIM
    nq = ATTN_HEADS * hd
    nkv = ATTN_KV_HEADS * hd

    def head_norm(xh, gain):
        ms = jnp.mean(xh * xh, axis=-1, keepdims=True)
        xh = (xh * lax.rsqrt(ms + RMS_EPS)) * gain
        if rope:
            xh = xh * cos_ref[...] + pltpu.roll(xh, hd // 2, 1) * sin_ref[...]
        return xh

    if want_q:
        q_ref, k_ref, v_ref = out_refs
        qf = jnp.dot(h, w_ref[:, 0:nq], preferred_element_type=F32)
        for i in range(ATTN_HEADS):
            xh = head_norm(qf[:, i * hd:(i + 1) * hd], qn_ref[...])
            q_ref[0, :, i * hd:(i + 1) * hd] = (xh * (hd ** -0.5)).astype(BF16)
    else:
        k_ref, v_ref = out_refs
    kf = jnp.dot(h, w_ref[:, nq:nq + nkv], preferred_element_type=F32)
    for i in range(ATTN_KV_HEADS):
        k_ref[0, :, i * hd:(i + 1) * hd] = head_norm(kf[:, i * hd:(i + 1) * hd], kn_ref[...]).astype(BF16)
    v_ref[0] = jnp.dot(h, w_ref[:, nq + nkv:], preferred_element_type=F32).astype(BF16)


def _proj1(x, nw, sc, sh, w, qn, kn, cos_t, sin_t, tm, rope, want_q):
    b, l, d = x.shape
    per_batch = sc.shape[0] > 1
    mod_map = (lambda i, j: (i, 0, 0)) if per_batch else (lambda i, j: (0, 0, 0))
    row = lambda i, j: (i, j, 0)
    const2 = lambda i, j: (0, 0)
    hd = ATTN_HEAD_DIM
    nq, nkv = ATTN_HEADS * hd, ATTN_KV_HEADS * hd
    out_specs = [pl.BlockSpec((1, tm, nkv), row), pl.BlockSpec((1, tm, nkv), row)]
    out_shape = [jax.ShapeDtypeStruct((b, l, nkv), BF16), jax.ShapeDtypeStruct((b, l, nkv), BF16)]
    if want_q:
        out_specs = [pl.BlockSpec((1, tm, nq), row)] + out_specs
        out_shape = [jax.ShapeDtypeStruct((b, l, nq), BF16)] + out_shape
    return pl.pallas_call(
        functools.partial(_proj1_kernel, rope=rope, want_q=want_q), grid=(b, l // tm),
        in_specs=[pl.BlockSpec((1, tm, d), row), pl.BlockSpec((1, d), const2),
                  pl.BlockSpec((1, 1, d), mod_map), pl.BlockSpec((1, 1, d), mod_map),
                  pl.BlockSpec(w.shape, const2), pl.BlockSpec((1, hd), const2),
                  pl.BlockSpec((1, hd), const2),
                  pl.BlockSpec((tm, hd), lambda i, j: (j, 0)), pl.BlockSpec((tm, hd), lambda i, j: (j, 0))],
        out_specs=out_specs, out_shape=out_shape,
        compiler_params=_cparams(("arbitrary", "arbitrary")),
        name="proj1")(x, nw, sc, sh, w, qn, kn, cos_t, sin_t)


def _attn_kernel(q_ref, kl_ref, vl_ref, kc_ref, vc_ref, o_ref):
    hd = ATTN_HEAD_DIM
    k_l, v_l, k_c, v_c = kl_ref[0], vl_ref[0], kc_ref[0], vc_ref[0]
    nt = (((1,), (1,)), ((), ()))
    for g in range(ATTN_GROUP):
        qh = q_ref[0, :, g * hd:(g + 1) * hd]
        s_l = lax.dot_general(qh, k_l, nt, preferred_element_type=F32)
        s_c = lax.dot_general(qh, k_c, nt, preferred_element_type=F32)
        m = jnp.maximum(jnp.max(s_l, axis=-1, keepdims=True), jnp.max(s_c, axis=-1, keepdims=True))
        p_l = jnp.exp(s_l - m)
        p_c = jnp.exp(s_c - m)
        den = jnp.sum(p_l, axis=-1, keepdims=True) + jnp.sum(p_c, axis=-1, keepdims=True)
        o = (jnp.dot(p_l.astype(BF16), v_l, preferred_element_type=F32)
             + jnp.dot(p_c.astype(BF16), v_c, preferred_element_type=F32))
        o_ref[0, :, g * hd:(g + 1) * hd] = (o / den).astype(BF16)


def _attention(q, k_l, v_l, k_c, v_c, tq):
    b, l, nq = q.shape
    n_ctx = k_c.shape[1]
    hd = ATTN_HEAD_DIM
    gw = ATTN_GROUP * hd
    return pl.pallas_call(
        _attn_kernel, grid=(b, ATTN_KV_HEADS, l // tq),
        in_specs=[pl.BlockSpec((1, tq, gw), lambda i, kv, j: (i, j, kv)),
                  pl.BlockSpec((1, l, hd), lambda i, kv, j: (i, 0, kv)),
                  pl.BlockSpec((1, l, hd), lambda i, kv, j: (i, 0, kv)),
                  pl.BlockSpec((1, n_ctx, hd), lambda i, kv, j: (i, 0, kv)),
                  pl.BlockSpec((1, n_ctx, hd), lambda i, kv, j: (i, 0, kv))],
        out_specs=pl.BlockSpec((1, tq, gw), lambda i, kv, j: (i, j, kv)),
        out_shape=jax.ShapeDtypeStruct((b, l, nq), BF16),
        compiler_params=_cparams(("arbitrary", "arbitrary", "arbitrary")),
        name="attention")(q, k_l, v_l, k_c, v_c)


def _row_tile(n, pref):
    return pref if n % pref == 0 else n


def _moe(streams, w_gate, w_up, w_down):
    d = streams[0][0].shape[1]
    t_all = sum(s[0].shape[0] for s in streams)
    n_blocks = -(-2 * t_all // MOE_BLOCK) + N_EXPERTS
    counts = sum(s[2] for s in streams)[0, :N_EXPERTS]
    pad_starts, blk_expert, blk_valid = _block_layout(counts, n_blocks)
    base = jnp.pad(pad_starts.astype(F32), (0, LANES - N_EXPERTS)).reshape(1, LANES)
    xs = jnp.zeros((n_blocks * MOE_BLOCK, d), F32)
    slot_tabs = []
    for h2, re, cnt in streams:
        slot_tab = _slots(re, base)
        xs = _scatter_rows(h2, slot_tab[:, :2], xs)
        slot_tabs.append(slot_tab)
        base = base + cnt
    ybuf = _experts(xs, blk_expert, blk_valid, w_gate, w_up, w_down)
    return ybuf, slot_tabs


def kernel(x, c, ctx, c_ctx, mod_w, mod_b, norm1, norm2, ab_w_in, ab_conv_a, ab_conv_qkv, ab_a_log,
           ab_dt_bias, ab_gnorm, ab_w_out, attn_w_qkv, attn_q_norm, attn_k_norm, attn_w_o, moe_w_group,
           moe_w_expert, moe_w_gate, moe_w_up, moe_w_down, final_norm):
    b, l, d = x.shape
    n_ctx = ctx.shape[1]
    assert mod_w.shape[0] == 2 and d == D_MODEL
    assert l % PRE_TILE == 0 and n_ctx % PRE_TILE == 0 and (l + n_ctx) % (SUPER * P1_UNITS) == 0
    tm_l = _row_tile(l, 512)
    tm_c = _row_tile(n_ctx, 256)

    rpad = (-(b + 1)) % 8
    cond = jnp.concatenate([c, c_ctx[None, :], jnp.zeros((rpad, d), F32)], axis=0)
    mod = _modulation(cond, mod_w, mod_b)

    def mods(i):
        lat = [mod[i, :b, k * d:(k + 1) * d].reshape(b, 1, d) for k in range(6)]
        cx = [mod[i, b:b + 1, k * d:(k + 1) * d].reshape(1, 1, d) for k in range(6)]
        return lat, cx

    def router_w(i):
        w = jnp.concatenate([moe_w_group[i], moe_w_expert[i]], axis=1)
        return jnp.pad(w, ((0, 0), (0, LANES - w.shape[1])))

    (sh1, sc1, g1, sh2, sc2, g2), (csh1, csc1, cg1, csh2, csc2, cg2) = mods(0)
    w_in = ab_w_in[0]
    n_gate = 4 * GDN_HEADS
    w0 = jnp.concatenate([w_in[:, :P0_A + QKV_WIDTH + P0_Z], w_in[:, -n_gate:],
                          jnp.zeros((d, LANES - n_gate), F32)], axis=1).astype(BF16)
    nw1 = norm1[0].reshape(1, d)
    a_l, qkv_l, z_l, gt_l = _proj0(x, nw1, sc1, sh1, w0, tm_l)
    a_c, qkv_c, z_c, gt_c = _proj0(ctx, nw1, csc1, csh1, w0, tm_c)

    def pair_gates(gt):
        n = gt.shape[1]
        g4 = gt[:, :, :n_gate].reshape(b, n, 4, GDN_PAIRS, 2)
        return jnp.transpose(g4, (0, 3, 1, 2, 4)).reshape(b, GDN_PAIRS, n, 8)

    def pair_vec(v):
        v2 = v.reshape(2, GDN_PAIRS, 2)
        z2 = jnp.zeros_like(v2)
        return jnp.stack([v2[0], z2[0], v2[1], z2[1]], axis=1).reshape(GDN_PAIRS, 1, 8)

    gnorm2 = jnp.tile(ab_gnorm[0], 2).reshape(1, LANES)
    ya_c, yb_c, ya_l, yb_l = _gdn_mixer(
        a_c, qkv_c, z_c, pair_gates(gt_c), a_l, qkv_l, z_l, pair_gates(gt_l),
        ab_conv_qkv[0], ab_conv_a[0], pair_vec(ab_a_log[0]), pair_vec(ab_dt_bias[0]), gnorm2)

    w_out = ab_w_out[0].astype(BF16)
    w_out_a, w_out_b = w_out[:A_WIDTH], w_out[A_WIDTH:]
    nw2 = norm2[0].reshape(1, d)
    wr0 = router_w(0)
    x1, h2_l, re_l, rw_l, cnt_l = _mix_out([ya_l, yb_l], [w_out_a, w_out_b], x, g1, nw2, sc2, sh2, wr0, tm_l)
    c1, h2_c, re_c, rw_c, cnt_c = _mix_out([ya_c, yb_c], [w_out_a, w_out_b], ctx, cg1, nw2, csc2, csh2, wr0,
                                           tm_c)

    t_l, t_c = b * l, b * n_ctx
    ybuf, (slot_l, slot_c) = _moe([(h2_l.reshape(t_l, d), re_l.reshape(t_l, LANES), cnt_l),
                                   (h2_c.reshape(t_c, d), re_c.reshape(t_c, LANES), cnt_c)],
                                  moe_w_gate[0], moe_w_up[0], moe_w_down[0])
    fn = final_norm.reshape(1, d)
    x2 = _combine(ybuf, slot_l, x1.reshape(t_l, d), g2, rw_l.reshape(t_l, LANES), fn,
                  l, _row_tile(l, 256), False).reshape(b, l, d)
    ctx2 = _combine(ybuf, slot_c, c1.reshape(t_c, d), cg2, rw_c.reshape(t_c, LANES), fn,
                    n_ctx, _row_tile(n_ctx, 256), False).reshape(b, n_ctx, d)

    (sh1, sc1, g1, sh2, sc2, g2), (csh1, csc1, cg1, csh2, csc2, cg2) = mods(1)
    hd = ATTN_HEAD_DIM
    perm = jnp.concatenate([jnp.arange(0, hd, 2), jnp.arange(1, hd, 2)])
    n_qk = (ATTN_HEADS + ATTN_KV_HEADS) * hd
    col_perm = (jnp.arange(n_qk).reshape(-1, hd)[:, perm]).reshape(-1)
    w_qkv = attn_w_qkv[0]
    w1 = jnp.concatenate([w_qkv[:, col_perm], w_qkv[:, n_qk:]], axis=1).astype(BF16)
    qn = attn_q_norm[0][perm].reshape(1, hd)
    kn = attn_k_norm[0][perm].reshape(1, hd)
    pos = jnp.arange(l, dtype=jnp.int32)
    n_freq = hd // 4
    inv = ROPE_THETA ** (-jnp.arange(n_freq, dtype=F32) / n_freq)
    ang = jnp.concatenate([(pos // GRID_W).astype(F32)[:, None] * inv,
                           (pos % GRID_W).astype(F32)[:, None] * inv], axis=-1)
    cos_t = jnp.concatenate([jnp.cos(ang), jnp.cos(ang)], axis=-1)
    sin_t = jnp.concatenate([-jnp.sin(ang), jnp.sin(ang)], axis=-1)
    nw1 = norm1[1].reshape(1, d)
    q_l, k_l, v_l = _proj1(x2, nw1, sc1, sh1, w1, qn, kn, cos_t, sin_t, tm_l, True, True)
    dummy = jnp.zeros((n_ctx, hd), F32)
    k_c, v_c = _proj1(ctx2, nw1, csc1, csh1, w1, qn, kn, dummy, dummy, tm_c, False, False)
    o = _attention(q_l, k_l, v_l, k_c, v_c, _row_tile(l, 256))

    nw2 = norm2[1].reshape(1, d)
    x3, h2_l, re_l, rw_l, cnt_l = _mix_out([o], [attn_w_o[0].astype(BF16)], x2, g1, nw2, sc2, sh2,
                                           router_w(1), tm_l)
    ybuf, (slot_l,) = _moe([(h2_l.reshape(t_l, d), re_l.reshape(t_l, LANES), cnt_l)],
                           moe_w_gate[1], moe_w_up[1], moe_w_down[1])
    out = _combine(ybuf, slot_l, x3.reshape(t_l, d), g2, rw_l.reshape(t_l, LANES), fn,
                   l, _row_tile(l, 256), True)
    return out.reshape(b, l, d)
```

```python
import functools

import jax
import jax.numpy as jnp
from jax import lax
from jax.experimental import pallas as pl
from jax.experimental.pallas import tpu as pltpu

F32 = jnp.float32
BF16 = jnp.bfloat16
HI = lax.Precision.HIGHEST

D_MODEL = 1024
RMS_EPS = 1e-6
L2_EPS = 1e-6
GRID_W = 64
A_WIDTH = 512
GDN_HEADS = 8
GDN_DK = 64
GDN_DV = 64
GDN_CHUNK = 64
QKV_WIDTH = 1536
ATTN_HEADS = 8
ATTN_KV_HEADS = 2
ATTN_GROUP = 4
ATTN_HEAD_DIM = 128
ROPE_THETA = 10000.0
N_GROUPS = 8
EXPERTS_PER_GROUP = 8
N_EXPERTS = 64
D_EXPERT = 384
MOE_BLOCK = 256

LANES = 128
GDN_PAIRS = GDN_HEADS // 2
SUPER = 2 * GDN_CHUNK
VMEM_LIMIT = 52 * 1024 * 1024


def _cparams(sem):
    return pltpu.CompilerParams(dimension_semantics=sem, vmem_limit_bytes=VMEM_LIMIT)


def _silu(x):
    return x * jax.nn.sigmoid(x)


def _norm_mod(x, nw, sc, sh):
    ms = jnp.mean(x * x, axis=-1, keepdims=True)
    return (x * lax.rsqrt(ms + RMS_EPS)) * nw * (1.0 + sc) + sh


def _bdot(a, b):
    return jnp.dot(a.astype(BF16), b.astype(BF16), preferred_element_type=F32)


def _bdot_nt(a, b):
    return lax.dot_general(a.astype(BF16), b.astype(BF16), (((1,), (1,)), ((), ())),
                           preferred_element_type=F32)


def _bdot_tn(a, b):
    return lax.dot_general(a.astype(BF16), b.astype(BF16), (((0,), (0,)), ((), ())),
                           preferred_element_type=F32)


def _mod_kernel(c_ref, w_ref, b_ref, o_ref):
    c = c_ref[...]
    o_ref[0] = jnp.dot(_silu(c), w_ref[0], precision=HI, preferred_element_type=F32) + b_ref[0]


def _modulation(cond, mod_w, mod_b):
    depth, d, n = mod_w.shape
    r = cond.shape[0]
    tn = 512
    return pl.pallas_call(
        _mod_kernel, grid=(depth, n // tn),
        in_specs=[pl.BlockSpec((r, d), lambda i, j: (0, 0)),
                  pl.BlockSpec((1, d, tn), lambda i, j: (i, 0, j)),
                  pl.BlockSpec((1, 1, tn), lambda i, j: (i, 0, j))],
        out_specs=pl.BlockSpec((1, r, tn), lambda i, j: (i, 0, j)),
        out_shape=jax.ShapeDtypeStruct((depth, r, n), F32),
        compiler_params=_cparams(("arbitrary", "arbitrary")),
        name="modulation")(cond, mod_w, mod_b.reshape(depth, 1, n))


P0_A = 3 * A_WIDTH
P0_Z = GDN_HEADS * GDN_DV
P0_COLS = P0_A + QKV_WIDTH + P0_Z + LANES


def _proj0_kernel(x_ref, nw_ref, sc_ref, sh_ref, w_ref, a_ref, qkv_ref, z_ref, g_ref):
    h = _norm_mod(x_ref[0], nw_ref[...], sc_ref[0], sh_ref[0]).astype(BF16)
    c0, c1, c2 = P0_A, P0_A + QKV_WIDTH, P0_A + QKV_WIDTH + P0_Z
    a_ref[0] = jnp.dot(h, w_ref[:, 0:c0], preferred_element_type=F32).astype(BF16)
    qkv_ref[0] = jnp.dot(h, w_ref[:, c0:c1], preferred_element_type=F32).astype(BF16)
    z_ref[0] = jnp.dot(h, w_ref[:, c1:c2], preferred_element_type=F32).astype(BF16)
    g_ref[0] = jnp.dot(h, w_ref[:, c2:], preferred_element_type=F32)


def _proj0(x, nw, sc, sh, w, tm):
    b, l, d = x.shape
    per_batch = sc.shape[0] > 1
    mod_map = (lambda i, j: (i, 0, 0)) if per_batch else (lambda i, j: (0, 0, 0))
    row = lambda i, j: (i, j, 0)
    return pl.pallas_call(
        _proj0_kernel, grid=(b, l // tm),
        in_specs=[pl.BlockSpec((1, tm, d), row),
                  pl.BlockSpec((1, d), lambda i, j: (0, 0)),
                  pl.BlockSpec((1, 1, d), mod_map),
                  pl.BlockSpec((1, 1, d), mod_map),
                  pl.BlockSpec((d, P0_COLS), lambda i, j: (0, 0))],
        out_specs=[pl.BlockSpec((1, tm, P0_A), row),
                   pl.BlockSpec((1, tm, QKV_WIDTH), row),
                   pl.BlockSpec((1, tm, P0_Z), row),
                   pl.BlockSpec((1, tm, LANES), row)],
        out_shape=[jax.ShapeDtypeStruct((b, l, P0_A), BF16),
                   jax.ShapeDtypeStruct((b, l, QKV_WIDTH), BF16),
                   jax.ShapeDtypeStruct((b, l, P0_Z), BF16),
                   jax.ShapeDtypeStruct((b, l, LANES), F32)],
        compiler_params=_cparams(("arbitrary", "arbitrary")),
        name="proj0")(x, nw, sc, sh, w)


PRE_TILE = 256
P1_UNITS = 2


def _gdn_kernel(qc_ref, kc_ref, vc_ref, ql_ref, kl_ref, vl_ref,
                xac_ref, gbc_ref, gcc_ref, xal_ref, gbl_ref, gcl_ref,
                zc_ref, zl_ref, gtc_ref, gtl_ref,
                cwq_ref, cwk_ref, cwv_ref, cwa_ref, a8_ref, dt8_ref, gn_ref,
                yac_ref, ybc_ref, yal_ref, ybl_ref,
                qn_s, kn_s, vn_s, g_s, bt_s, m_s, qq_s, qe_s, ou_s, gl_s, o_s,
                *, n_ctx, n_lat):
    n_tot = n_ctx + n_lat
    lane = lax.broadcasted_iota(jnp.int32, (1, LANES), 1)
    lo = lane < GDN_DK
    lo2 = (lax.broadcasted_iota(jnp.int32, (1, 2 * LANES), 1) & (LANES - 1)) < GDN_DK
    head_blk = (lax.broadcasted_iota(jnp.int32, (LANES, LANES), 0) // GDN_DK
                == lax.broadcasted_iota(jnp.int32, (LANES, LANES), 1) // GDN_DK)
    blk_ones = jnp.where(head_blk, 1.0, 0.0).astype(F32)

    def conv3_tile(ref, r0, n, w_ref, mul_ref=None):
        def rows(start, size):
            v = ref[0, pl.ds(start, size), :].astype(F32)
            if mul_ref is not None:
                v = v * mul_ref[0, pl.ds(start, size), :].astype(F32)
            return v
        cur = rows(r0, PRE_TILE)
        prev = rows(pl.multiple_of(jnp.maximum(r0 - 16, 0), 16), 16)[15:16, :] * (r0 > 0).astype(F32)
        nxt = (rows(pl.multiple_of(jnp.minimum(r0 + PRE_TILE, n - 16), 16), 16)[0:1, :]
               * (r0 + PRE_TILE < n).astype(F32))
        row = lax.broadcasted_iota(jnp.int32, (PRE_TILE, 1), 0)
        xm = jnp.where(row == 0, prev, pltpu.roll(cur, 1, 0))
        xp = jnp.where(row == PRE_TILE - 1, nxt, pltpu.roll(cur, PRE_TILE - 1, 0))
        return xm * w_ref[0:1, :] + cur * w_ref[1:2, :] + xp * w_ref[2:3, :]

    def head_sumsq(x):
        return jnp.dot(x * x, blk_ones, precision=HI, preferred_element_type=F32)

    def prepass(seg_off, n, q_ref, k_ref, v_ref, gt_ref):
        def body(t, carry):
            r0 = pl.multiple_of(t * PRE_TILE, PRE_TILE)
            dst = pl.ds(pl.multiple_of(seg_off + r0, PRE_TILE), PRE_TILE)
            q = _silu(conv3_tile(q_ref, r0, n, cwq_ref))
            k = _silu(conv3_tile(k_ref, r0, n, cwk_ref))
            v = _silu(conv3_tile(v_ref, r0, n, cwv_ref))
            qn_s[dst, :] = q * lax.rsqrt(head_sumsq(q) + L2_EPS) * (GDN_DK ** -0.5)
            kn_s[dst, :] = k * lax.rsqrt(head_sumsq(k) + L2_EPS)
            vn_s[dst, :] = v
            gt = gt_ref[0, 0, pl.ds(r0, PRE_TILE), :]
            sp_in = gt + dt8_ref[0]
            softplus = jnp.maximum(sp_in, 0.0) + jnp.log1p(jnp.exp(-jnp.abs(sp_in)))
            graw = -jnp.exp(a8_ref[0]) * softplus
            beta = jax.nn.sigmoid(gt)
            for d in range(2):
                c = 4 * d
                g_s[d, dst, :] = jnp.where(lo, graw[:, c:c + 1], graw[:, c + 1:c + 2])
                bt_s[d, dst, :] = jnp.where(lo, beta[:, c + 2:c + 3], beta[:, c + 3:c + 4])
            return carry
        lax.fori_loop(0, n // PRE_TILE, body, 0)

    prepass(0, n_ctx, qc_ref, kc_ref, vc_ref, gtc_ref)
    prepass(n_ctx, n_lat, ql_ref, kl_ref, vl_ref, gtl_ref)

    ii = lax.broadcasted_iota(jnp.int32, (SUPER, SUPER), 0)
    jj = lax.broadcasted_iota(jnp.int32, (SUPER, SUPER), 1)
    xr = ii ^ jj
    same_chunk = xr < GDN_CHUNK
    top = lax.broadcasted_iota(jnp.int32, (SUPER, 1), 0) < GDN_CHUNK
    eye = jnp.where(ii == jj, 1.0, 0.0).astype(F32)
    cs_all = jnp.where(same_chunk, 1.0, 0.0).astype(F32)

    chains = ((0, 0), (0, 1), (1, 0), (1, 1))

    zero_b = jnp.zeros((SUPER, SUPER), BF16)

    def setup_unit(r0):
        rows = pl.ds(r0, SUPER)
        q_s = qn_s[rows, :]
        k_s = kn_s[rows, :]
        v_s = vn_s[rows, :]
        kb = k_s.astype(BF16)
        gram, qk = [], []
        for hh in range(2):
            hm = lo if hh == 0 else jnp.logical_not(lo)
            gram.append(_bdot_nt(jnp.where(hm, k_s, 0.0), kb))
            qk.append(_bdot_nt(jnp.where(hm, q_s, 0.0), kb))
        valid, b_l, gc, gcl = [], [], [], []
        for d in range(2):
            tri = (jj <= ii) if d == 0 else (jj >= ii)
            valid.append(jnp.logical_and(same_chunk, tri))
            g_l = g_s[d, rows, :]
            b_l.append(bt_s[d, rows, :])
            gc.append(jnp.dot(jnp.where(valid[d], 1.0, 0.0).astype(F32), g_l, precision=HI,
                              preferred_element_type=F32))
            gcl.append(jnp.dot(cs_all, g_l, precision=HI, preferred_element_type=F32))
        u = dict(r0=r0, rows=rows, qg=[], kg=[], rhs=[], nm=[], att=[], t_inv=[])
        gct = []
        for d in range(2):
            egc = jnp.exp(gc[d])
            u['qg'].append(q_s * egc)
            u['kg'].append((k_s * jnp.exp(gcl[d] - gc[d])).astype(BF16))
            gl_s[d, rows, :] = jnp.exp(gcl[d])
            u['rhs'].append(jnp.concatenate([v_s * b_l[d], k_s * b_l[d] * egc], axis=1).astype(BF16))
            gct.append(gc[d].T)
        for d, hh in chains:
            c0 = hh * GDN_DK
            diff = gc[d][:, c0:c0 + 1] - gct[d][c0:c0 + 1, :]
            dec = jnp.exp(jnp.where(valid[d], diff, -jnp.inf))
            u['nm'].append(jnp.where(ii == jj, 0.0, b_l[d][:, c0:c0 + 1] * gram[hh] * dec).astype(BF16))
            u['att'].append((qk[hh] * dec).astype(BF16))
            u['t_inv'].append(eye - jnp.where(xr < 2, u['nm'][-1].astype(F32), 0.0))
        return u

    def finish_unit(u, uw, auw):
        for d in range(2):
            c_a, c_b = 2 * d, 2 * d + 1
            uw_d = jnp.where(lo2, uw[c_a], uw[c_b])
            auw_d = jnp.where(lo2, auw[c_a], auw[c_b])
            ou_s[d, u['rows'], :] = auw_d[:, :LANES]
            qe_s[d, u['rows'], :] = (u['qg'][d] - auw_d[:, LANES:]).astype(BF16)
            uw_b = uw_d.astype(BF16)
            kg = u['kg'][d]
            for half in range(2):
                keep = top if half == 0 else jnp.logical_not(top)
                mq = _bdot_tn(jnp.where(keep, kg, jnp.zeros_like(kg)), uw_b)
                crow = pl.ds(pl.multiple_of(2 * u['r0'] + half * LANES, LANES), LANES)
                qq_s[d, crow, :] = jnp.where(head_blk, mq[:, :LANES], 0.0)
                m_s[d, crow, :] = jnp.where(head_blk, mq[:, LANES:], 0.0).astype(BF16)

    def phase1(s, carry):
        units = [setup_unit(pl.multiple_of((s * P1_UNITS + k) * SUPER, SUPER)) for k in range(P1_UNITS)]
        nm = [m for u in units for m in u['nm']]
        t_inv = [t for u in units for t in u['t_inv']]
        nc = len(nm)
        for lvl in range(1, 6):
            sel = (xr >> lvl) == 1
            tb = [t.astype(BF16) for t in t_inv]
            ot = [jnp.dot(jnp.where(sel, nm[c], zero_b), tb[c], preferred_element_type=F32) for c in range(nc)]
            tot = [jnp.dot(tb[c], ot[c].astype(BF16), preferred_element_type=F32) for c in range(nc)]
            t_inv = [t_inv[c] - tot[c] for c in range(nc)]
        rhs = [u['rhs'][d] for u in units for d, _ in chains]
        att = [a for u in units for a in u['att']]
        uw = [jnp.dot(t_inv[c].astype(BF16), rhs[c], preferred_element_type=F32) for c in range(nc)]
        auw = [jnp.dot(att[c], uw[c].astype(BF16), preferred_element_type=F32) for c in range(nc)]
        for k, u in enumerate(units):
            finish_unit(u, uw[4 * k:4 * k + 4], auw[4 * k:4 * k + 4])
        return carry

    lax.fori_loop(0, n_tot // (SUPER * P1_UNITS), phase1, 0)

    n_ch = n_tot // GDN_CHUNK
    n_cc = n_ctx // GDN_CHUNK

    def scan_step(n, states):
        c_f = n
        c_b = jnp.where(n < n_cc, n_cc - 1 - n, n_ch - 1 + n_cc - n)
        new_states = []
        for d, c in ((0, c_f), (1, c_b)):
            st = states[d]
            rows = pl.ds(pl.multiple_of(c * GDN_CHUNK, GDN_CHUNK), GDN_CHUNK)
            crow = pl.ds(pl.multiple_of(c * LANES, LANES), LANES)
            lhs = jnp.concatenate([m_s[d, crow, :], qe_s[d, rows, :]], axis=0)
            res = jnp.dot(lhs, st.astype(BF16), preferred_element_type=F32)
            o_s[d, rows, :] = res[LANES:] + ou_s[d, rows, :]
            gl = gl_s[d, pl.ds(pl.multiple_of(c * GDN_CHUNK, GDN_CHUNK), 1), :]
            new_states.append(st * gl + qq_s[d, crow, :] - res[:LANES])
        return tuple(new_states)

    s0 = jnp.zeros((LANES, LANES), F32)
    lax.fori_loop(0, n_ch, scan_step, (s0, s0))

    def postpass(seg_off, n, z_ref, xa_ref, gb_ref, gcv_ref, ya_ref, yb_ref):
        def body(t, carry):
            r0 = pl.multiple_of(t * PRE_TILE, PRE_TILE)
            src = pl.ds(pl.multiple_of(seg_off + r0, PRE_TILE), PRE_TILE)
            dst = pl.ds(r0, PRE_TILE)
            o = o_s[0, src, :] + o_s[1, src, :]
            ms = head_sumsq(o) * (1.0 / GDN_DV)
            z = z_ref[0, dst, :].astype(F32)
            yb_ref[0, dst, :] = ((o * lax.rsqrt(ms + RMS_EPS)) * gn_ref[...] * _silu(z)).astype(BF16)
            conv = conv3_tile(gcv_ref, r0, n, cwa_ref, mul_ref=xa_ref)
            ya_ref[0, dst, :] = (gb_ref[0, dst, :].astype(F32) * conv).astype(BF16)
            return carry
        lax.fori_loop(0, n // PRE_TILE, body, 0)

    postpass(0, n_ctx, zc_ref, xac_ref, gbc_ref, gcc_ref, yac_ref, ybc_ref)
    postpass(n_ctx, n_lat, zl_ref, xal_ref, gbl_ref, gcl_ref, yal_ref, ybl_ref)


def _gdn_mixer(a_c, qkv_c, z_c, gt_c, a_l, qkv_l, z_l, gt_l, conv_qkv, conv_a, a8, dt8, gnorm2):
    b, n_ctx, _ = qkv_c.shape
    n_lat = qkv_l.shape[1]
    n_tot = n_ctx + n_lat
    np_ = GDN_PAIRS

    def seq(n, off):
        return pl.BlockSpec((1, n, LANES), lambda i, p, off=off: (i, 0, off + p))

    def cw(off):
        return pl.BlockSpec((3, LANES), lambda i, p, off=off: (0, off + p))

    gate = lambda n: pl.BlockSpec((1, 1, n, 8), lambda i, p: (i, p, 0, 0))
    vec8 = pl.BlockSpec((1, 1, 8), lambda i, p: (p, 0, 0))
    in_specs = [seq(n_ctx, 0), seq(n_ctx, np_), seq(n_ctx, 2 * np_),
                seq(n_lat, 0), seq(n_lat, np_), seq(n_lat, 2 * np_),
                seq(n_ctx, 0), seq(n_ctx, np_), seq(n_ctx, 2 * np_),
                seq(n_lat, 0), seq(n_lat, np_), seq(n_lat, 2 * np_),
                seq(n_ctx, 0), seq(n_lat, 0), gate(n_ctx), gate(n_lat),
                cw(0), cw(np_), cw(2 * np_), cw(0), vec8, vec8,
                pl.BlockSpec((1, LANES), lambda i, p: (0, 0))]
    out_specs = [seq(n_ctx, 0), seq(n_ctx, 0), seq(n_lat, 0), seq(n_lat, 0)]
    out_shape = [jax.ShapeDtypeStruct((b, n_ctx, A_WIDTH), BF16),
                 jax.ShapeDtypeStruct((b, n_ctx, GDN_HEADS * GDN_DV), BF16),
                 jax.ShapeDtypeStruct((b, n_lat, A_WIDTH), BF16),
                 jax.ShapeDtypeStruct((b, n_lat, GDN_HEADS * GDN_DV), BF16)]
    f32buf = lambda *lead: pltpu.VMEM((*lead, n_tot, LANES), F32)
    scratch = [f32buf(), f32buf(), f32buf(), f32buf(2), f32buf(2),
               pltpu.VMEM((2, 2 * n_tot, LANES), BF16), pltpu.VMEM((2, 2 * n_tot, LANES), F32),
               pltpu.VMEM((2, n_tot, LANES), BF16), f32buf(2), f32buf(2), f32buf(2)]
    return pl.pallas_call(
        functools.partial(_gdn_kernel, n_ctx=n_ctx, n_lat=n_lat),
        grid=(b, np_), in_specs=in_specs, out_specs=out_specs, out_shape=out_shape,
        scratch_shapes=scratch,
        compiler_params=_cparams(("arbitrary", "arbitrary")),
        name="gdn_mixer")(qkv_c, qkv_c, qkv_c, qkv_l, qkv_l, qkv_l,
                          a_c, a_c, a_c, a_l, a_l, a_l, z_c, z_l, gt_c, gt_l,
                          conv_qkv, conv_qkv, conv_qkv, conv_a, a8, dt8, gnorm2)


def _route(logits):
    lane = lax.broadcasted_iota(jnp.int32, logits.shape, 1)
    lane_f = lane.astype(F32)
    neg = -jnp.inf
    is_g = lane < N_GROUPS
    gl = jnp.where(is_g, logits, neg)
    gmax = jnp.max(gl, axis=-1, keepdims=True)
    grp = jnp.min(jnp.where(gl == gmax, lane_f, float(LANES)), axis=-1, keepdims=True)
    gsum = jnp.sum(jnp.where(is_g, jnp.exp(gl - gmax), 0.0), axis=-1, keepdims=True)
    gprob = 1.0 / gsum
    first = float(N_GROUPS) + grp * float(EXPERTS_PER_GROUP)
    ing = jnp.logical_and(lane_f >= first, lane_f < first + float(EXPERTS_PER_GROUP))
    el = jnp.where(ing, logits, neg)
    v1 = jnp.max(el, axis=-1, keepdims=True)
    i1 = jnp.min(jnp.where(el == v1, lane_f, float(LANES)), axis=-1, keepdims=True)
    el2 = jnp.where(lane_f == i1, neg, el)
    v2 = jnp.max(el2, axis=-1, keepdims=True)
    i2 = jnp.min(jnp.where(el2 == v2, lane_f, float(LANES)), axis=-1, keepdims=True)
    t = jnp.exp(v2 - v1)
    den = 1.0 + t
    w1 = (1.0 / den) * gprob
    w2 = (t / den) * gprob
    e1 = i1 - float(N_GROUPS)
    e2 = i2 - float(N_GROUPS)
    e_out = jnp.where(lane == 0, e1, jnp.where(lane == 1, e2, 0.0))
    w_out = jnp.where(lane == 0, w1, jnp.where(lane == 1, w2, 0.0))
    picked = jnp.where(jnp.logical_or(lane_f == e1, lane_f == e2), 1.0, 0.0)
    return e_out.astype(jnp.int32), w_out, jnp.sum(picked, axis=0, keepdims=True)


def _mix_out_kernel(*refs, n_parts):
    ys = refs[:n_parts]
    ws = refs[n_parts:2 * n_parts]
    x_ref, g1_ref, nw_ref, sc_ref, sh_ref, wr_ref = refs[2 * n_parts:2 * n_parts + 6]
    xo_ref, h2_ref, re_ref, rw_ref, cnt_ref = refs[2 * n_parts + 6:]
    y = jnp.dot(ys[0][0], ws[0][...], preferred_element_type=F32)
    for k in range(1, n_parts):
        y = y + jnp.dot(ys[k][0], ws[k][...], preferred_element_type=F32)
    x = x_ref[0] + g1_ref[0] * y
    xo_ref[0] = x
    h2 = _norm_mod(x, nw_ref[...], sc_ref[0], sh_ref[0])
    h2_ref[0] = h2
    logits = jnp.dot(h2, wr_ref[...], precision=HI, preferred_element_type=F32)
    e_out, w_out, cnt = _route(logits)
    re_ref[0] = e_out
    rw_ref[0] = w_out

    @pl.when(jnp.logical_and(pl.program_id(0) == 0, pl.program_id(1) == 0))
    def _():
        cnt_ref[...] = jnp.zeros_like(cnt_ref)

    cnt_ref[...] += cnt


def _mix_out(parts, weights, x, g1, nw, sc, sh, w_router, tm):
    b, l, d = x.shape
    n_parts = len(parts)
    per_batch = g1.shape[0] > 1
    mod_map = (lambda i, j: (i, 0, 0)) if per_batch else (lambda i, j: (0, 0, 0))
    row = lambda i, j: (i, j, 0)
    const2 = lambda i, j: (0, 0)
    in_specs = ([pl.BlockSpec((1, tm, p.shape[2]), row) for p in parts]
                + [pl.BlockSpec(w.shape, const2) for w in weights]
                + [pl.BlockSpec((1, tm, d), row), pl.BlockSpec((1, 1, d), mod_map),
                   pl.BlockSpec((1, d), const2), pl.BlockSpec((1, 1, d), mod_map),
                   pl.BlockSpec((1, 1, d), mod_map), pl.BlockSpec((d, LANES), const2)])
    return pl.pallas_call(
        functools.partial(_mix_out_kernel, n_parts=n_parts), grid=(b, l // tm),
        in_specs=in_specs,
        out_specs=[pl.BlockSpec((1, tm, d), row), pl.BlockSpec((1, tm, d), row),
                   pl.BlockSpec((1, tm, LANES), row), pl.BlockSpec((1, tm, LANES), row),
                   pl.BlockSpec((1, LANES), const2)],
        out_shape=[jax.ShapeDtypeStruct((b, l, d), F32), jax.ShapeDtypeStruct((b, l, d), F32),
                   jax.ShapeDtypeStruct((b, l, LANES), jnp.int32),
                   jax.ShapeDtypeStruct((b, l, LANES), F32),
                   jax.ShapeDtypeStruct((1, LANES), F32)],
        compiler_params=_cparams(("arbitrary", "arbitrary")),
        name="mix_out")(*parts, *weights, x, g1, nw, sc, sh, w_router)


def _block_layout(counts, n_blocks):
    counts = counts.astype(jnp.int32)
    padded = (counts + MOE_BLOCK - 1) // MOE_BLOCK * MOE_BLOCK
    pad_ends = jnp.cumsum(padded)
    pad_starts = pad_ends - padded
    blk_start = jnp.arange(n_blocks, dtype=jnp.int32) * MOE_BLOCK
    blk_expert = jnp.minimum(jnp.sum((blk_start[:, None] >= pad_ends[None, :]).astype(jnp.int32), axis=1),
                             N_EXPERTS - 1).astype(jnp.int32)
    blk_valid = (blk_start < pad_ends[-1]).astype(jnp.int32)
    return pad_starts, blk_expert, blk_valid


SLOT_TILE = 512


def _slot_kernel(re_ref, base_ref, slot_ref, carry):
    @pl.when(pl.program_id(0) == 0)
    def _():
        carry[...] = jnp.zeros_like(carry)

    e = re_ref[...]
    r = e.shape[0]
    lane = lax.broadcasted_iota(jnp.int32, e.shape, 1)
    oh0 = lane == e[:, 0:1]
    oh1 = lane == e[:, 1:2]
    both = jnp.where(jnp.logical_or(oh0, oh1), 1.0, 0.0)
    strict = (lax.broadcasted_iota(jnp.int32, (r, r), 0) > lax.broadcasted_iota(jnp.int32, (r, r), 1))
    before = jnp.dot(jnp.where(strict, 1.0, 0.0).astype(BF16), both.astype(BF16),
                     preferred_element_type=F32)
    pos = before + carry[...] + base_ref[...]
    s0 = jnp.sum(jnp.where(oh0, pos, 0.0), axis=-1, keepdims=True)
    s1 = jnp.sum(jnp.where(oh1, pos, 0.0), axis=-1, keepdims=True)
    slot_ref[...] = jnp.where(lane == 0, s0, jnp.where(lane == 1, s1, 0.0)).astype(jnp.int32)
    carry[...] += jnp.sum(both, axis=0, keepdims=True)


def _slots(re, base):
    t = re.shape[0]
    r = _row_tile(t, SLOT_TILE)
    return pl.pallas_call(
        _slot_kernel, grid=(t // r,),
        in_specs=[pl.BlockSpec((r, LANES), lambda i: (i, 0)), pl.BlockSpec((1, LANES), lambda i: (0, 0))],
        out_specs=pl.BlockSpec((r, LANES), lambda i: (i, 0)),
        out_shape=jax.ShapeDtypeStruct((t, LANES), jnp.int32),
        scratch_shapes=[pltpu.VMEM((1, LANES), F32)],
        compiler_params=_cparams(("arbitrary",)),
        name="slots")(re, base)


SCATTER_TILE = 256
SLAB = 8


def _scatter_kernel(idx_ref, h_ref, xs_in, xs_out, stage, idx_smem, ssem, isem, *, rt):
    del xs_in
    i = pl.program_id(0)
    n = pl.num_programs(0)
    slot = i % 2

    def drain(s):
        def body(r, carry):
            pltpu.make_async_copy(stage.at[s, pl.ds(0, SLAB)], xs_out.at[pl.ds(0, SLAB)], ssem.at[s]).wait()
            return carry
        lax.fori_loop(0, 2 * rt, body, 0, unroll=8)

    @pl.when(i > 1)
    def _():
        drain(slot)

    cp = pltpu.make_async_copy(idx_ref.at[0], idx_smem.at[pl.ds(slot, 1)], isem.at[0])
    cp.start()
    for s in range(SLAB):
        stage[slot, pl.ds(s, rt, stride=SLAB), :] = h_ref[:, s * LANES:(s + 1) * LANES]
    cp.wait()

    for r in range(rt):
        src = stage.at[slot, pl.ds(r * SLAB, SLAB)]
        d0 = pl.multiple_of(idx_smem[slot, r] * SLAB, SLAB)
        d1 = pl.multiple_of(idx_smem[slot, rt + r] * SLAB, SLAB)
        pltpu.make_async_copy(src, xs_out.at[pl.ds(d0, SLAB)], ssem.at[slot]).start()
        pltpu.make_async_copy(src, xs_out.at[pl.ds(d1, SLAB)], ssem.at[slot]).start()

    @pl.when(i == n - 1)
    def _():
        drain(slot)

        @pl.when(n > 1)
        def _():
            drain(1 - slot)


def _scatter_rows(h2, slot_of, xs):
    t, d = h2.shape
    assert d == SLAB * LANES
    rt = _row_tile(t, SCATTER_TILE)
    nt = t // rt
    idx = jnp.concatenate([slot_of[:, 0].reshape(nt, 1, rt), slot_of[:, 1].reshape(nt, 1, rt)], axis=2)
    return pl.pallas_call(
        functools.partial(_scatter_kernel, rt=rt), grid=(nt,),
        in_specs=[pl.BlockSpec((1, 1, 2 * rt), lambda i: (i, 0, 0)),
                  pl.BlockSpec((rt, d), lambda i: (i, 0)), pl.BlockSpec(memory_space=pl.ANY)],
        out_specs=pl.BlockSpec(memory_space=pl.ANY),
        out_shape=jax.ShapeDtypeStruct(xs.shape, xs.dtype),
        input_output_aliases={2: 0},
        scratch_shapes=[pltpu.VMEM((2, rt * SLAB, LANES), F32), pltpu.SMEM((2, 2 * rt), jnp.int32),
                        pltpu.SemaphoreType.DMA((2,)), pltpu.SemaphoreType.DMA((1,))],
        compiler_params=_cparams(("arbitrary",)),
        name="scatter_rows")(idx, h2, xs)


def _gather_rows(idx_vmem_ref, idx_smem, slot, src_hbm, dst_buf, isem, gsem, n_rows):
    cp = pltpu.make_async_copy(idx_vmem_ref.at[0], idx_smem.at[pl.ds(slot, 1)], isem.at[0])
    cp.start()
    cp.wait()

    for r in range(n_rows):
        tok = pl.multiple_of(idx_smem[slot, r] * SLAB, SLAB)
        pltpu.make_async_copy(src_hbm.at[pl.ds(tok, SLAB)], dst_buf.at[slot, pl.ds(r * SLAB, SLAB)],
                              gsem.at[slot]).start()


def _wait_rows(src_hbm, dst_buf, slot, gsem, n_rows):
    def body(r, carry):
        pltpu.make_async_copy(src_hbm.at[pl.ds(0, SLAB)], dst_buf.at[slot, pl.ds(0, SLAB)],
                              gsem.at[slot]).wait()
        return carry
    lax.fori_loop(0, n_rows, body, 0, unroll=8)


def _rows_from_slabs(ref, lead, first, n):
    return jnp.concatenate([ref[(*lead, pl.ds(first * SLAB + s, n, stride=SLAB), slice(None))]
                            for s in range(SLAB)], axis=1)


def _expert_kernel(be_ref, bv_ref, x_ref, wg_ref, wu_ref, wd_ref, out_ref):
    valid = bv_ref[pl.program_id(0)] > 0

    @pl.when(valid)
    def _():
        x = _rows_from_slabs(x_ref, (), 0, MOE_BLOCK).astype(BF16)
        g = jnp.dot(x, wg_ref[0].astype(BF16), preferred_element_type=F32)
        u = jnp.dot(x, wu_ref[0].astype(BF16), preferred_element_type=F32)
        h = (_silu(g) * u).astype(BF16)
        y = jnp.dot(h, wd_ref[0].astype(BF16), preferred_element_type=F32)
        for s in range(SLAB):
            out_ref[pl.ds(s, MOE_BLOCK, stride=SLAB), :] = y[:, s * LANES:(s + 1) * LANES]

    @pl.when(jnp.logical_not(valid))
    def _():
        out_ref[...] = jnp.zeros_like(out_ref)


def _experts(xs, blk_expert, blk_valid, w_gate, w_up, w_down):
    d = SLAB * LANES
    n_blocks = xs.shape[0] // (MOE_BLOCK * SLAB)
    blk = pl.BlockSpec((MOE_BLOCK * SLAB, LANES), lambda i, be, bv: (i, 0))
    grid_spec = pltpu.PrefetchScalarGridSpec(
        num_scalar_prefetch=2, grid=(n_blocks,),
        in_specs=[blk,
                  pl.BlockSpec((1, d, D_EXPERT), lambda i, be, bv: (be[i], 0, 0)),
                  pl.BlockSpec((1, d, D_EXPERT), lambda i, be, bv: (be[i], 0, 0)),
                  pl.BlockSpec((1, D_EXPERT, d), lambda i, be, bv: (be[i], 0, 0))],
        out_specs=blk)
    return pl.pallas_call(
        _expert_kernel, grid_spec=grid_spec,
        out_shape=jax.ShapeDtypeStruct(xs.shape, F32),
        compiler_params=_cparams(("arbitrary",)),
        name="experts")(blk_expert, blk_valid, xs, w_gate, w_up, w_down)


def _combine_kernel(idx_ref, idx_nxt_ref, y_hbm, x_ref, g2_ref, rw_ref, fn_ref, out_ref,
                    rbuf, idx_smem, gsem, isem, *, tm, final_norm):
    i = pl.program_id(0)
    n = pl.num_programs(0)
    slot = i % 2

    @pl.when(i == 0)
    def _():
        _gather_rows(idx_ref, idx_smem, 0, y_hbm, rbuf, isem, gsem, 2 * tm)

    @pl.when(i + 1 < n)
    def _():
        _gather_rows(idx_nxt_ref, idx_smem, 1 - slot, y_hbm, rbuf, isem, gsem, 2 * tm)

    _wait_rows(y_hbm, rbuf, slot, gsem, 2 * tm)
    rw = rw_ref[...]
    y = (rw[:, 0:1] * _rows_from_slabs(rbuf, (slot,), 0, tm)
         + rw[:, 1:2] * _rows_from_slabs(rbuf, (slot,), tm, tm))
    x = x_ref[...] + g2_ref[0] * y
    if final_norm:
        ms = jnp.mean(x * x, axis=-1, keepdims=True)
        x = (x * lax.rsqrt(ms + RMS_EPS)) * fn_ref[...]
    out_ref[...] = x


def _combine(ybuf, slot_of, x, g2, rw, fn, rows_per_batch, tm, final_norm):
    t, d = x.shape
    nt = t // tm
    idx = jnp.concatenate([slot_of[:, 0].reshape(nt, 1, tm), slot_of[:, 1].reshape(nt, 1, tm)], axis=2)
    per_batch = g2.shape[0] > 1
    tpb = rows_per_batch // tm
    mod_map = (lambda i: (i // tpb, 0, 0)) if per_batch else (lambda i: (0, 0, 0))
    return pl.pallas_call(
        functools.partial(_combine_kernel, tm=tm, final_norm=final_norm), grid=(nt,),
        in_specs=[pl.BlockSpec((1, 1, 2 * tm), lambda i: (i, 0, 0)),
                  pl.BlockSpec((1, 1, 2 * tm), lambda i: (jnp.minimum(i + 1, nt - 1), 0, 0)),
                  pl.BlockSpec(memory_space=pl.ANY),
                  pl.BlockSpec((tm, d), lambda i: (i, 0)),
                  pl.BlockSpec((1, 1, d), mod_map),
                  pl.BlockSpec((tm, LANES), lambda i: (i, 0)),
                  pl.BlockSpec((1, d), lambda i: (0, 0))],
        out_specs=pl.BlockSpec((tm, d), lambda i: (i, 0)),
        out_shape=jax.ShapeDtypeStruct((t, d), F32),
        scratch_shapes=[pltpu.VMEM((2, 2 * tm * SLAB, LANES), F32),
                        pltpu.SMEM((2, 2 * tm), jnp.int32),
                        pltpu.SemaphoreType.DMA((2,)),
                        pltpu.SemaphoreType.DMA((1,))],
        compiler_params=_cparams(("arbitrary",)),
        name="combine")(idx, idx, ybuf, x, g2, rw, fn)


def _proj1_kernel(x_ref, nw_ref, sc_ref, sh_ref, w_ref, qn_ref, kn_ref, cos_ref, sin_ref,
                  *out_refs, rope, want_q):
    h = _norm_mod(x_ref[0], nw_ref[...], sc_ref[0], sh_ref[0]).astype(BF16)
    hd = ATTN_HEAD_DIM
    nq = ATTN_HEADS * hd
    nkv = ATTN_KV_HEADS * hd

    def head_norm(xh, gain):
        ms = jnp.mean(xh * xh, axis=-1, keepdims=True)
        xh = (xh * lax.rsqrt(ms + RMS_EPS)) * gain
        if rope:
            xh = xh * cos_ref[...] + pltpu.roll(xh, hd // 2, 1) * sin_ref[...]
        return xh

    if want_q:
        q_ref, k_ref, v_ref = out_refs
        qf = jnp.dot(h, w_ref[:, 0:nq], preferred_element_type=F32)
        for i in range(ATTN_HEADS):
            xh = head_norm(qf[:, i * hd:(i + 1) * hd], qn_ref[...])
            q_ref[0, :, i * hd:(i + 1) * hd] = (xh * (hd ** -0.5)).astype(BF16)
    else:
        k_ref, v_ref = out_refs
    kf = jnp.dot(h, w_ref[:, nq:nq + nkv], preferred_element_type=F32)
    for i in range(ATTN_KV_HEADS):
        k_ref[0, :, i * hd:(i + 1) * hd] = head_norm(kf[:, i * hd:(i + 1) * hd], kn_ref[...]).astype(BF16)
    v_ref[0] = jnp.dot(h, w_ref[:, nq + nkv:], preferred_element_type=F32).astype(BF16)


def _proj1(x, nw, sc, sh, w, qn, kn, cos_t, sin_t, tm, rope, want_q):
    b, l, d = x.shape
    per_batch = sc.shape[0] > 1
    mod_map = (lambda i, j: (i, 0, 0)) if per_batch else (lambda i, j: (0, 0, 0))
    row = lambda i, j: (i, j, 0)
    const2 = lambda i, j: (0, 0)
    hd = ATTN_HEAD_DIM
    nq, nkv = ATTN_HEADS * hd, ATTN_KV_HEADS * hd
    out_specs = [pl.BlockSpec((1, tm, nkv), row), pl.BlockSpec((1, tm, nkv), row)]
    out_shape = [jax.ShapeDtypeStruct((b, l, nkv), BF16), jax.ShapeDtypeStruct((b, l, nkv), BF16)]
    if want_q:
        out_specs = [pl.BlockSpec((1, tm, nq), row)] + out_specs
        out_shape = [jax.ShapeDtypeStruct((b, l, nq), BF16)] + out_shape
    return pl.pallas_call(
        functools.partial(_proj1_kernel, rope=rope, want_q=want_q), grid=(b, l // tm),
        in_specs=[pl.BlockSpec((1, tm, d), row), pl.BlockSpec((1, d), const2),
                  pl.BlockSpec((1, 1, d), mod_map), pl.BlockSpec((1, 1, d), mod_map),
                  pl.BlockSpec(w.shape, const2), pl.BlockSpec((1, hd), const2),
                  pl.BlockSpec((1, hd), const2),
                  pl.BlockSpec((tm, hd), lambda i, j: (j, 0)), pl.BlockSpec((tm, hd), lambda i, j: (j, 0))],
        out_specs=out_specs, out_shape=out_shape,
        compiler_params=_cparams(("arbitrary", "arbitrary")),
        name="proj1")(x, nw, sc, sh, w, qn, kn, cos_t, sin_t)


def _attn_kernel(q_ref, kl_ref, vl_ref, kc_ref, vc_ref, o_ref):
    hd = ATTN_HEAD_DIM
    k_l, v_l, k_c, v_c = kl_ref[0], vl_ref[0], kc_ref[0], vc_ref[0]
    nt = (((1,), (1,)), ((), ()))
    for g in range(ATTN_GROUP):
        qh = q_ref[0, :, g * hd:(g + 1) * hd]
        s_l = lax.dot_general(qh, k_l, nt, preferred_element_type=F32)
        s_c = lax.dot_general(qh, k_c, nt, preferred_element_type=F32)
        m = jnp.maximum(jnp.max(s_l, axis=-1, keepdims=True), jnp.max(s_c, axis=-1, keepdims=True))
        p_l = jnp.exp(s_l - m)
        p_c = jnp.exp(s_c - m)
        den = jnp.sum(p_l, axis=-1, keepdims=True) + jnp.sum(p_c, axis=-1, keepdims=True)
        o = (jnp.dot(p_l.astype(BF16), v_l, preferred_element_type=F32)
             + jnp.dot(p_c.astype(BF16), v_c, preferred_element_type=F32))
        o_ref[0, :, g * hd:(g + 1) * hd] = (o / den).astype(BF16)


def _attention(q, k_l, v_l, k_c, v_c, tq):
    b, l, nq = q.shape
    n_ctx = k_c.shape[1]
    hd = ATTN_HEAD_DIM
    gw = ATTN_GROUP * hd
    return pl.pallas_call(
        _attn_kernel, grid=(b, ATTN_KV_HEADS, l // tq),
        in_specs=[pl.BlockSpec((1, tq, gw), lambda i, kv, j: (i, j, kv)),
                  pl.BlockSpec((1, l, hd), lambda i, kv, j: (i, 0, kv)),
                  pl.BlockSpec((1, l, hd), lambda i, kv, j: (i, 0, kv)),
                  pl.BlockSpec((1, n_ctx, hd), lambda i, kv, j: (i, 0, kv)),
                  pl.BlockSpec((1, n_ctx, hd), lambda i, kv, j: (i, 0, kv))],
        out_specs=pl.BlockSpec((1, tq, gw), lambda i, kv, j: (i, j, kv)),
        out_shape=jax.ShapeDtypeStruct((b, l, nq), BF16),
        compiler_params=_cparams(("arbitrary", "arbitrary", "arbitrary")),
        name="attention")(q, k_l, v_l, k_c, v_c)


def _row_tile(n, pref):
    return pref if n % pref == 0 else n


def _moe(streams, w_gate, w_up, w_down):
    d = streams[0][0].shape[1]
    t_all = sum(s[0].shape[0] for s in streams)
    n_blocks = -(-2 * t_all // MOE_BLOCK) + N_EXPERTS
    counts = sum(s[2] for s in streams)[0, :N_EXPERTS]
    pad_starts, blk_expert, blk_valid = _block_layout(counts, n_blocks)
    base = jnp.pad(pad_starts.astype(F32), (0, LANES - N_EXPERTS)).reshape(1, LANES)
    xs = jnp.zeros((n_blocks * MOE_BLOCK * SLAB, d // SLAB), F32)
    slot_tabs = []
    for h2, re, cnt in streams:
        slot_tab = _slots(re, base)
        xs = _scatter_rows(h2, slot_tab[:, :2], xs)
        slot_tabs.append(slot_tab)
        base = base + cnt
    ybuf = _experts(xs, blk_expert, blk_valid, w_gate, w_up, w_down)
    return ybuf, slot_tabs


def kernel(x, c, ctx, c_ctx, mod_w, mod_b, norm1, norm2, ab_w_in, ab_conv_a, ab_conv_qkv, ab_a_log,
           ab_dt_bias, ab_gnorm, ab_w_out, attn_w_qkv, attn_q_norm, attn_k_norm, attn_w_o, moe_w_group,
           moe_w_expert, moe_w_gate, moe_w_up, moe_w_down, final_norm):
    b, l, d = x.shape
    n_ctx = ctx.shape[1]
    assert mod_w.shape[0] == 2 and d == D_MODEL
    assert l % PRE_TILE == 0 and n_ctx % PRE_TILE == 0 and (l + n_ctx) % (SUPER * P1_UNITS) == 0
    tm_l = _row_tile(l, 512)
    tm_c = _row_tile(n_ctx, 256)

    rpad = (-(b + 1)) % 8
    cond = jnp.concatenate([c, c_ctx[None, :], jnp.zeros((rpad, d), F32)], axis=0)
    mod = _modulation(cond, mod_w, mod_b)

    def mods(i):
        lat = [mod[i, :b, k * d:(k + 1) * d].reshape(b, 1, d) for k in range(6)]
        cx = [mod[i, b:b + 1, k * d:(k + 1) * d].reshape(1, 1, d) for k in range(6)]
        return lat, cx

    def router_w(i):
        w = jnp.concatenate([moe_w_group[i], moe_w_expert[i]], axis=1)
        return jnp.pad(w, ((0, 0), (0, LANES - w.shape[1])))

    (sh1, sc1, g1, sh2, sc2, g2), (csh1, csc1, cg1, csh2, csc2, cg2) = mods(0)
    w_in = ab_w_in[0]
    n_gate = 4 * GDN_HEADS
    w0 = jnp.concatenate([w_in[:, :P0_A + QKV_WIDTH + P0_Z], w_in[:, -n_gate:],
                          jnp.zeros((d, LANES - n_gate), F32)], axis=1).astype(BF16)
    nw1 = norm1[0].reshape(1, d)
    a_l, qkv_l, z_l, gt_l = _proj0(x, nw1, sc1, sh1, w0, tm_l)
    a_c, qkv_c, z_c, gt_c = _proj0(ctx, nw1, csc1, csh1, w0, tm_c)

    def pair_gates(gt):
        n = gt.shape[1]
        g4 = gt[:, :, :n_gate].reshape(b, n, 4, GDN_PAIRS, 2)
        return jnp.transpose(g4, (0, 3, 1, 2, 4)).reshape(b, GDN_PAIRS, n, 8)

    def pair_vec(v):
        v2 = v.reshape(2, GDN_PAIRS, 2)
        z2 = jnp.zeros_like(v2)
        return jnp.stack([v2[0], z2[0], v2[1], z2[1]], axis=1).reshape(GDN_PAIRS, 1, 8)

    gnorm2 = jnp.tile(ab_gnorm[0], 2).reshape(1, LANES)
    ya_c, yb_c, ya_l, yb_l = _gdn_mixer(
        a_c, qkv_c, z_c, pair_gates(gt_c), a_l, qkv_l, z_l, pair_gates(gt_l),
        ab_conv_qkv[0], ab_conv_a[0], pair_vec(ab_a_log[0]), pair_vec(ab_dt_bias[0]), gnorm2)

    w_out = ab_w_out[0].astype(BF16)
    w_out_a, w_out_b = w_out[:A_WIDTH], w_out[A_WIDTH:]
    nw2 = norm2[0].reshape(1, d)
    wr0 = router_w(0)
    x1, h2_l, re_l, rw_l, cnt_l = _mix_out([ya_l, yb_l], [w_out_a, w_out_b], x, g1, nw2, sc2, sh2, wr0, tm_l)
    c1, h2_c, re_c, rw_c, cnt_c = _mix_out([ya_c, yb_c], [w_out_a, w_out_b], ctx, cg1, nw2, csc2, csh2, wr0,
                                           tm_c)

    t_l, t_c = b * l, b * n_ctx
    ybuf, (slot_l, slot_c) = _moe([(h2_l.reshape(t_l, d), re_l.reshape(t_l, LANES), cnt_l),
                                   (h2_c.reshape(t_c, d), re_c.reshape(t_c, LANES), cnt_c)],
                                  moe_w_gate[0], moe_w_up[0], moe_w_down[0])
    fn = final_norm.reshape(1, d)
    x2 = _combine(ybuf, slot_l, x1.reshape(t_l, d), g2, rw_l.reshape(t_l, LANES), fn,
                  l, _row_tile(l, 256), False).reshape(b, l, d)
    ctx2 = _combine(ybuf, slot_c, c1.reshape(t_c, d), cg2, rw_c.reshape(t_c, LANES), fn,
                    n_ctx, _row_tile(n_ctx, 256), False).reshape(b, n_ctx, d)

    (sh1, sc1, g1, sh2, sc2, g2), (csh1, csc1, cg1, csh2, csc2, cg2) = mods(1)
    hd = ATTN_HEAD_DIM
    perm = jnp.concatenate([jnp.arange(0, hd, 2), jnp.arange(1, hd, 2)])
    n_qk = (ATTN_HEADS + ATTN_KV_HEADS) * hd
    col_perm = (jnp.arange(n_qk).reshape(-1, hd)[:, perm]).reshape(-1)
    w_qkv = attn_w_qkv[0]
    w1 = jnp.concatenate([w_qkv[:, col_perm], w_qkv[:, n_qk:]], axis=1).astype(BF16)
    qn = attn_q_norm[0][perm].reshape(1, hd)
    kn = attn_k_norm[0][perm].reshape(1, hd)
    pos = jnp.arange(l, dtype=jnp.int32)
    n_freq = hd // 4
    inv = ROPE_THETA ** (-jnp.arange(n_freq, dtype=F32) / n_freq)
    ang = jnp.concatenate([(pos // GRID_W).astype(F32)[:, None] * inv,
                           (pos % GRID_W).astype(F32)[:, None] * inv], axis=-1)
    cos_t = jnp.concatenate([jnp.cos(ang), jnp.cos(ang)], axis=-1)
    sin_t = jnp.concatenate([-jnp.sin(ang), jnp.sin(ang)], axis=-1)
    nw1 = norm1[1].reshape(1, d)
    q_l, k_l, v_l = _proj1(x2, nw1, sc1, sh1, w1, qn, kn, cos_t, sin_t, tm_l, True, True)
    dummy = jnp.zeros((n_ctx, hd), F32)
    k_c, v_c = _proj1(ctx2, nw1, csc1, csh1, w1, qn, kn, dummy, dummy, tm_c, False, False)
    o = _attention(q_l, k_l, v_l, k_c, v_c, _row_tile(l, 256))

    nw2 = norm2[1].reshape(1, d)
    x3, h2_l, re_l, rw_l, cnt_l = _mix_out([o], [attn_w_o[0].astype(BF16)], x2, g1, nw2, sc2, sh2,
                                           router_w(1), tm_l)
    ybuf, (slot_l,) = _moe([(h2_l.reshape(t_l, d), re_l.reshape(t_l, LANES), cnt_l)],
                           moe_w_gate[1], moe_w_up[1], moe_w_down[1])
    out = _combine(ybuf, slot_l, x3.reshape(t_l, d), g2, rw_l.reshape(t_l, LANES), fn,
                   l, _row_tile(l, 256), True)
    return out.reshape(b, l, d)
```

```python
import functools

import jax
import jax.numpy as jnp
from jax import lax
from jax.experimental import pallas as pl
from jax.experimental.pallas import tpu as pltpu

F32 = jnp.float32
BF16 = jnp.bfloat16
HI = lax.Precision.HIGHEST

D_MODEL = 1024
RMS_EPS = 1e-6
L2_EPS = 1e-6
GRID_W = 64
A_WIDTH = 512
GDN_HEADS = 8
GDN_DK = 64
GDN_DV = 64
GDN_CHUNK = 64
QKV_WIDTH = 1536
ATTN_HEADS = 8
ATTN_KV_HEADS = 2
ATTN_GROUP = 4
ATTN_HEAD_DIM = 128
ROPE_THETA = 10000.0
N_GROUPS = 8
EXPERTS_PER_GROUP = 8
N_EXPERTS = 64
D_EXPERT = 384
MOE_BLOCK = 256

LANES = 128
GDN_PAIRS = GDN_HEADS // 2
SUPER = 2 * GDN_CHUNK
VMEM_LIMIT = 52 * 1024 * 1024


def _cparams(sem):
    return pltpu.CompilerParams(dimension_semantics=sem, vmem_limit_bytes=VMEM_LIMIT)


def _silu(x):
    return x * jax.nn.sigmoid(x)


def _norm_mod(x, nw, sc, sh):
    ms = jnp.mean(x * x, axis=-1, keepdims=True)
    return (x * lax.rsqrt(ms + RMS_EPS)) * nw * (1.0 + sc) + sh


def _bdot(a, b):
    return jnp.dot(a.astype(BF16), b.astype(BF16), preferred_element_type=F32)


def _bdot_nt(a, b):
    return lax.dot_general(a.astype(BF16), b.astype(BF16), (((1,), (1,)), ((), ())),
                           preferred_element_type=F32)


def _bdot_tn(a, b):
    return lax.dot_general(a.astype(BF16), b.astype(BF16), (((0,), (0,)), ((), ())),
                           preferred_element_type=F32)


def _mod_kernel(c_ref, w_ref, b_ref, o_ref):
    c = c_ref[...]
    o_ref[0] = jnp.dot(_silu(c), w_ref[0], precision=HI, preferred_element_type=F32) + b_ref[0]


def _modulation(cond, mod_w, mod_b):
    depth, d, n = mod_w.shape
    r = cond.shape[0]
    tn = 512
    return pl.pallas_call(
        _mod_kernel, grid=(depth, n // tn),
        in_specs=[pl.BlockSpec((r, d), lambda i, j: (0, 0)),
                  pl.BlockSpec((1, d, tn), lambda i, j: (i, 0, j)),
                  pl.BlockSpec((1, 1, tn), lambda i, j: (i, 0, j))],
        out_specs=pl.BlockSpec((1, r, tn), lambda i, j: (i, 0, j)),
        out_shape=jax.ShapeDtypeStruct((depth, r, n), F32),
        compiler_params=_cparams(("arbitrary", "arbitrary")),
        name="modulation")(cond, mod_w, mod_b.reshape(depth, 1, n))


P0_A = 3 * A_WIDTH
P0_Z = GDN_HEADS * GDN_DV
P0_COLS = P0_A + QKV_WIDTH + P0_Z + LANES


def _proj0_kernel(x_ref, nw_ref, sc_ref, sh_ref, w_ref, a_ref, qkv_ref, z_ref, g_ref):
    h = _norm_mod(x_ref[0], nw_ref[...], sc_ref[0], sh_ref[0]).astype(BF16)
    c0, c1, c2 = P0_A, P0_A + QKV_WIDTH, P0_A + QKV_WIDTH + P0_Z
    a_ref[0] = jnp.dot(h, w_ref[:, 0:c0], preferred_element_type=F32).astype(BF16)
    qkv_ref[0] = jnp.dot(h, w_ref[:, c0:c1], preferred_element_type=F32).astype(BF16)
    z_ref[0] = jnp.dot(h, w_ref[:, c1:c2], preferred_element_type=F32).astype(BF16)
    g_ref[0] = jnp.dot(h, w_ref[:, c2:], preferred_element_type=F32)


def _proj0(x, nw, sc, sh, w, tm):
    b, l, d = x.shape
    per_batch = sc.shape[0] > 1
    mod_map = (lambda i, j: (i, 0, 0)) if per_batch else (lambda i, j: (0, 0, 0))
    row = lambda i, j: (i, j, 0)
    return pl.pallas_call(
        _proj0_kernel, grid=(b, l // tm),
        in_specs=[pl.BlockSpec((1, tm, d), row),
                  pl.BlockSpec((1, d), lambda i, j: (0, 0)),
                  pl.BlockSpec((1, 1, d), mod_map),
                  pl.BlockSpec((1, 1, d), mod_map),
                  pl.BlockSpec((d, P0_COLS), lambda i, j: (0, 0))],
        out_specs=[pl.BlockSpec((1, tm, P0_A), row),
                   pl.BlockSpec((1, tm, QKV_WIDTH), row),
                   pl.BlockSpec((1, tm, P0_Z), row),
                   pl.BlockSpec((1, tm, LANES), row)],
        out_shape=[jax.ShapeDtypeStruct((b, l, P0_A), BF16),
                   jax.ShapeDtypeStruct((b, l, QKV_WIDTH), BF16),
                   jax.ShapeDtypeStruct((b, l, P0_Z), BF16),
                   jax.ShapeDtypeStruct((b, l, LANES), F32)],
        compiler_params=_cparams(("arbitrary", "arbitrary")),
        name="proj0")(x, nw, sc, sh, w)


PRE_TILE = 256
P1_UNITS = 3


def _gdn_kernel(qc_ref, kc_ref, vc_ref, ql_ref, kl_ref, vl_ref,
                xac_ref, gbc_ref, gcc_ref, xal_ref, gbl_ref, gcl_ref,
                zc_ref, zl_ref, gtc_ref, gtl_ref,
                cwq_ref, cwk_ref, cwv_ref, cwa_ref, a8_ref, dt8_ref, gn_ref,
                yac_ref, ybc_ref, yal_ref, ybl_ref,
                qn_s, kn_s, vn_s, g_s, bt_s, m_s, qq_s, qe_s, ou_s, gl_s, o_s,
                *, n_ctx, n_lat):
    n_tot = n_ctx + n_lat
    lane = lax.broadcasted_iota(jnp.int32, (1, LANES), 1)
    lo = lane < GDN_DK
    lo2 = (lax.broadcasted_iota(jnp.int32, (1, 2 * LANES), 1) & (LANES - 1)) < GDN_DK
    head_blk = (lax.broadcasted_iota(jnp.int32, (LANES, LANES), 0) // GDN_DK
                == lax.broadcasted_iota(jnp.int32, (LANES, LANES), 1) // GDN_DK)
    blk_ones = jnp.where(head_blk, 1.0, 0.0).astype(BF16)

    def conv3_tile(ref, r0, n, w_ref, mul_ref=None):
        def rows(start, size):
            v = ref[0, pl.ds(start, size), :].astype(F32)
            if mul_ref is not None:
                v = v * mul_ref[0, pl.ds(start, size), :].astype(F32)
            return v
        cur = rows(r0, PRE_TILE)
        prev = rows(pl.multiple_of(jnp.maximum(r0 - 16, 0), 16), 16)[15:16, :] * (r0 > 0).astype(F32)
        nxt = (rows(pl.multiple_of(jnp.minimum(r0 + PRE_TILE, n - 16), 16), 16)[0:1, :]
               * (r0 + PRE_TILE < n).astype(F32))
        row = lax.broadcasted_iota(jnp.int32, (PRE_TILE, 1), 0)
        xm = jnp.where(row == 0, prev, pltpu.roll(cur, 1, 0))
        xp = jnp.where(row == PRE_TILE - 1, nxt, pltpu.roll(cur, PRE_TILE - 1, 0))
        return xm * w_ref[0:1, :] + cur * w_ref[1:2, :] + xp * w_ref[2:3, :]

    def head_sumsq(x):
        x2 = x * x
        hi = x2.astype(BF16)
        lo = (x2 - hi.astype(F32)).astype(BF16)
        return (jnp.dot(lo, blk_ones, preferred_element_type=F32)
                + jnp.dot(hi, blk_ones, preferred_element_type=F32))

    def prepass(seg_off, n, q_ref, k_ref, v_ref, gt_ref):
        def body(t, carry):
            r0 = pl.multiple_of(t * PRE_TILE, PRE_TILE)
            dst = pl.ds(pl.multiple_of(seg_off + r0, PRE_TILE), PRE_TILE)
            q = _silu(conv3_tile(q_ref, r0, n, cwq_ref))
            k = _silu(conv3_tile(k_ref, r0, n, cwk_ref))
            v = _silu(conv3_tile(v_ref, r0, n, cwv_ref))
            qn_s[dst, :] = q * lax.rsqrt(head_sumsq(q) + L2_EPS) * (GDN_DK ** -0.5)
            kn_s[dst, :] = k * lax.rsqrt(head_sumsq(k) + L2_EPS)
            vn_s[dst, :] = v
            gt = gt_ref[0, 0, pl.ds(r0, PRE_TILE), :]
            sp_in = gt + dt8_ref[0]
            softplus = jnp.maximum(sp_in, 0.0) + jnp.log1p(jnp.exp(-jnp.abs(sp_in)))
            graw = -jnp.exp(a8_ref[0]) * softplus
            beta = jax.nn.sigmoid(gt)
            for d in range(2):
                c = 4 * d
                g_s[d, dst, :] = jnp.where(lo, graw[:, c:c + 1], graw[:, c + 1:c + 2])
                bt_s[d, dst, :] = jnp.where(lo, beta[:, c + 2:c + 3], beta[:, c + 3:c + 4])
            return carry
        lax.fori_loop(0, n // PRE_TILE, body, 0)

    prepass(0, n_ctx, qc_ref, kc_ref, vc_ref, gtc_ref)
    prepass(n_ctx, n_lat, ql_ref, kl_ref, vl_ref, gtl_ref)

    ii = lax.broadcasted_iota(jnp.int32, (SUPER, SUPER), 0)
    jj = lax.broadcasted_iota(jnp.int32, (SUPER, SUPER), 1)
    xr = ii ^ jj
    same_chunk = xr < GDN_CHUNK
    top = lax.broadcasted_iota(jnp.int32, (SUPER, 1), 0) < GDN_CHUNK
    eye = jnp.where(ii == jj, 1.0, 0.0).astype(F32)
    cs_all = jnp.where(same_chunk, 1.0, 0.0).astype(F32)

    chains = ((0, 0), (0, 1), (1, 0), (1, 1))

    zero_b = jnp.zeros((SUPER, SUPER), BF16)

    def setup_unit(r0):
        rows = pl.ds(r0, SUPER)
        q_s = qn_s[rows, :]
        k_s = kn_s[rows, :]
        v_s = vn_s[rows, :]
        kb = k_s.astype(BF16)
        gram, qk = [], []
        for hh in range(2):
            hm = lo if hh == 0 else jnp.logical_not(lo)
            gram.append(_bdot_nt(jnp.where(hm, k_s, 0.0), kb))
            qk.append(_bdot_nt(jnp.where(hm, q_s, 0.0), kb))
        valid = [jnp.logical_and(same_chunk, jj <= ii), jnp.logical_and(same_chunk, jj >= ii)]
        b_l = [bt_s[0, rows, :], bt_s[1, rows, :]]
        sel = jnp.concatenate([jnp.where(valid[0], 1.0, 0.0), jnp.where(valid[1], 1.0, 0.0), cs_all],
                              axis=0).astype(BF16)
        g_rem = jnp.concatenate([g_s[0, rows, :], g_s[1, rows, :]], axis=1)
        parts = []
        for _ in range(3):
            part = g_rem.astype(BF16)
            parts.append(jnp.dot(sel, part, preferred_element_type=F32))
            g_rem = g_rem - part.astype(F32)
        cums = (parts[2] + parts[1]) + parts[0]
        gc = [cums[0:SUPER, 0:LANES], cums[SUPER:2 * SUPER, LANES:]]
        gcl = [cums[2 * SUPER:, 0:LANES], cums[2 * SUPER:, LANES:]]
        u = dict(r0=r0, rows=rows, qg=[], kg=[], rhs=[], nm=[], att=[], t_inv=[])
        gct = []
        for d in range(2):
            egc = jnp.exp(gc[d])
            u['qg'].append(q_s * egc)
            u['kg'].append((k_s * jnp.exp(gcl[d] - gc[d])).astype(BF16))
            gl_s[d, rows, :] = jnp.exp(gcl[d])
            u['rhs'].append(jnp.concatenate([v_s * b_l[d], k_s * b_l[d] * egc], axis=1).astype(BF16))
            gct.append(gc[d].T)
        for d, hh in chains:
            c0 = hh * GDN_DK
            diff = gc[d][:, c0:c0 + 1] - gct[d][c0:c0 + 1, :]
            dec = jnp.exp(jnp.where(valid[d], diff, -jnp.inf))
            u['nm'].append(jnp.where(ii == jj, 0.0, b_l[d][:, c0:c0 + 1] * gram[hh] * dec).astype(BF16))
            u['att'].append((qk[hh] * dec).astype(BF16))
            u['t_inv'].append(eye - jnp.where(xr < 2, u['nm'][-1].astype(F32), 0.0))
        return u

    def finish_unit(u, uw, auw):
        for d in range(2):
            c_a, c_b = 2 * d, 2 * d + 1
            uw_d = jnp.where(lo2, uw[c_a], uw[c_b])
            auw_d = jnp.where(lo2, auw[c_a], auw[c_b])
            ou_s[d, u['rows'], :] = auw_d[:, :LANES]
            qe_s[d, u['rows'], :] = (u['qg'][d] - auw_d[:, LANES:]).astype(BF16)
            uw_b = uw_d.astype(BF16)
            kg = u['kg'][d]
            for half in range(2):
                keep = top if half == 0 else jnp.logical_not(top)
                mq = _bdot_tn(jnp.where(keep, kg, jnp.zeros_like(kg)), uw_b)
                crow = pl.ds(pl.multiple_of(2 * u['r0'] + half * LANES, LANES), LANES)
                qq_s[d, crow, :] = jnp.where(head_blk, mq[:, :LANES], 0.0)
                m_s[d, crow, :] = jnp.where(head_blk, mq[:, LANES:], 0.0).astype(BF16)

    def phase1(s, carry):
        units = [setup_unit(pl.multiple_of((s * P1_UNITS + k) * SUPER, SUPER)) for k in range(P1_UNITS)]
        nm = [m for u in units for m in u['nm']]
        t_inv = [t for u in units for t in u['t_inv']]
        nc = len(nm)
        for lvl in range(1, 6):
            sel = (xr >> lvl) == 1
            tb = [t.astype(BF16) for t in t_inv]
            ot = [jnp.dot(jnp.where(sel, nm[c], zero_b), tb[c], preferred_element_type=F32) for c in range(nc)]
            tot = [jnp.dot(tb[c], ot[c].astype(BF16), preferred_element_type=F32) for c in range(nc)]
            t_inv = [t_inv[c] - tot[c] for c in range(nc)]
        rhs = [u['rhs'][d] for u in units for d, _ in chains]
        att = [a for u in units for a in u['att']]
        uw = [jnp.dot(t_inv[c].astype(BF16), rhs[c], preferred_element_type=F32) for c in range(nc)]
        auw = [jnp.dot(att[c], uw[c].astype(BF16), preferred_element_type=F32) for c in range(nc)]
        for k, u in enumerate(units):
            finish_unit(u, uw[4 * k:4 * k + 4], auw[4 * k:4 * k + 4])
        return carry

    lax.fori_loop(0, n_tot // (SUPER * P1_UNITS), phase1, 0)

    n_ch = n_tot // GDN_CHUNK
    n_cc = n_ctx // GDN_CHUNK

    def scan_step(n, states):
        c_f = n
        c_b = jnp.where(n < n_cc, n_cc - 1 - n, n_ch - 1 + n_cc - n)
        new_states = []
        for d, c in ((0, c_f), (1, c_b)):
            st = states[d]
            rows = pl.ds(pl.multiple_of(c * GDN_CHUNK, GDN_CHUNK), GDN_CHUNK)
            crow = pl.ds(pl.multiple_of(c * LANES, LANES), LANES)
            lhs = jnp.concatenate([m_s[d, crow, :], qe_s[d, rows, :]], axis=0)
            res = jnp.dot(lhs, st.astype(BF16), preferred_element_type=F32)
            o_s[d, rows, :] = res[LANES:] + ou_s[d, rows, :]
            gl = gl_s[d, pl.ds(pl.multiple_of(c * GDN_CHUNK, GDN_CHUNK), 1), :]
            new_states.append(st * gl + qq_s[d, crow, :] - res[:LANES])
        return tuple(new_states)

    s0 = jnp.zeros((LANES, LANES), F32)
    lax.fori_loop(0, n_ch, scan_step, (s0, s0))

    def postpass(seg_off, n, z_ref, xa_ref, gb_ref, gcv_ref, ya_ref, yb_ref):
        def body(t, carry):
            r0 = pl.multiple_of(t * PRE_TILE, PRE_TILE)
            src = pl.ds(pl.multiple_of(seg_off + r0, PRE_TILE), PRE_TILE)
            dst = pl.ds(r0, PRE_TILE)
            o = o_s[0, src, :] + o_s[1, src, :]
            ms = head_sumsq(o) * (1.0 / GDN_DV)
            z = z_ref[0, dst, :].astype(F32)
            yb_ref[0, dst, :] = ((o * lax.rsqrt(ms + RMS_EPS)) * gn_ref[...] * _silu(z)).astype(BF16)
            conv = conv3_tile(gcv_ref, r0, n, cwa_ref, mul_ref=xa_ref)
            ya_ref[0, dst, :] = (gb_ref[0, dst, :].astype(F32) * conv).astype(BF16)
            return carry
        lax.fori_loop(0, n // PRE_TILE, body, 0)

    postpass(0, n_ctx, zc_ref, xac_ref, gbc_ref, gcc_ref, yac_ref, ybc_ref)
    postpass(n_ctx, n_lat, zl_ref, xal_ref, gbl_ref, gcl_ref, yal_ref, ybl_ref)


def _gdn_mixer(a_c, qkv_c, z_c, gt_c, a_l, qkv_l, z_l, gt_l, conv_qkv, conv_a, a8, dt8, gnorm2):
    b, n_ctx, _ = qkv_c.shape
    n_lat = qkv_l.shape[1]
    n_tot = n_ctx + n_lat
    np_ = GDN_PAIRS

    def seq(n, off):
        return pl.BlockSpec((1, n, LANES), lambda i, p, off=off: (i, 0, off + p))

    def cw(off):
        return pl.BlockSpec((3, LANES), lambda i, p, off=off: (0, off + p))

    gate = lambda n: pl.BlockSpec((1, 1, n, 8), lambda i, p: (i, p, 0, 0))
    vec8 = pl.BlockSpec((1, 1, 8), lambda i, p: (p, 0, 0))
    in_specs = [seq(n_ctx, 0), seq(n_ctx, np_), seq(n_ctx, 2 * np_),
                seq(n_lat, 0), seq(n_lat, np_), seq(n_lat, 2 * np_),
                seq(n_ctx, 0), seq(n_ctx, np_), seq(n_ctx, 2 * np_),
                seq(n_lat, 0), seq(n_lat, np_), seq(n_lat, 2 * np_),
                seq(n_ctx, 0), seq(n_lat, 0), gate(n_ctx), gate(n_lat),
                cw(0), cw(np_), cw(2 * np_), cw(0), vec8, vec8,
                pl.BlockSpec((1, LANES), lambda i, p: (0, 0))]
    out_specs = [seq(n_ctx, 0), seq(n_ctx, 0), seq(n_lat, 0), seq(n_lat, 0)]
    out_shape = [jax.ShapeDtypeStruct((b, n_ctx, A_WIDTH), BF16),
                 jax.ShapeDtypeStruct((b, n_ctx, GDN_HEADS * GDN_DV), BF16),
                 jax.ShapeDtypeStruct((b, n_lat, A_WIDTH), BF16),
                 jax.ShapeDtypeStruct((b, n_lat, GDN_HEADS * GDN_DV), BF16)]
    f32buf = lambda *lead: pltpu.VMEM((*lead, n_tot, LANES), F32)
    scratch = [f32buf(), f32buf(), f32buf(), f32buf(2), f32buf(2),
               pltpu.VMEM((2, 2 * n_tot, LANES), BF16), pltpu.VMEM((2, 2 * n_tot, LANES), F32),
               pltpu.VMEM((2, n_tot, LANES), BF16), f32buf(2), f32buf(2), f32buf(2)]
    return pl.pallas_call(
        functools.partial(_gdn_kernel, n_ctx=n_ctx, n_lat=n_lat),
        grid=(b, np_), in_specs=in_specs, out_specs=out_specs, out_shape=out_shape,
        scratch_shapes=scratch,
        compiler_params=_cparams(("arbitrary", "arbitrary")),
        name="gdn_mixer")(qkv_c, qkv_c, qkv_c, qkv_l, qkv_l, qkv_l,
                          a_c, a_c, a_c, a_l, a_l, a_l, z_c, z_l, gt_c, gt_l,
                          conv_qkv, conv_qkv, conv_qkv, conv_a, a8, dt8, gnorm2)


def _route(logits):
    lane = lax.broadcasted_iota(jnp.int32, logits.shape, 1)
    lane_f = lane.astype(F32)
    neg = -jnp.inf
    is_g = lane < N_GROUPS
    gl = jnp.where(is_g, logits, neg)
    gmax = jnp.max(gl, axis=-1, keepdims=True)
    grp = jnp.min(jnp.where(gl == gmax, lane_f, float(LANES)), axis=-1, keepdims=True)
    gsum = jnp.sum(jnp.where(is_g, jnp.exp(gl - gmax), 0.0), axis=-1, keepdims=True)
    gprob = 1.0 / gsum
    first = float(N_GROUPS) + grp * float(EXPERTS_PER_GROUP)
    ing = jnp.logical_and(lane_f >= first, lane_f < first + float(EXPERTS_PER_GROUP))
    el = jnp.where(ing, logits, neg)
    v1 = jnp.max(el, axis=-1, keepdims=True)
    i1 = jnp.min(jnp.where(el == v1, lane_f, float(LANES)), axis=-1, keepdims=True)
    el2 = jnp.where(lane_f == i1, neg, el)
    v2 = jnp.max(el2, axis=-1, keepdims=True)
    i2 = jnp.min(jnp.where(el2 == v2, lane_f, float(LANES)), axis=-1, keepdims=True)
    t = jnp.exp(v2 - v1)
    den = 1.0 + t
    w1 = (1.0 / den) * gprob
    w2 = (t / den) * gprob
    e1 = i1 - float(N_GROUPS)
    e2 = i2 - float(N_GROUPS)
    e_out = jnp.where(lane == 0, e1, jnp.where(lane == 1, e2, 0.0))
    w_out = jnp.where(lane == 0, w1, jnp.where(lane == 1, w2, 0.0))
    picked = jnp.where(jnp.logical_or(lane_f == e1, lane_f == e2), 1.0, 0.0)
    return e_out.astype(jnp.int32), w_out, jnp.sum(picked, axis=0, keepdims=True)


def _mix_out_kernel(*refs, n_parts):
    ys = refs[:n_parts]
    ws = refs[n_parts:2 * n_parts]
    x_ref, g1_ref, nw_ref, sc_ref, sh_ref, wr_ref = refs[2 * n_parts:2 * n_parts + 6]
    xo_ref, h2_ref, re_ref, rw_ref, cnt_ref = refs[2 * n_parts + 6:]
    y = jnp.dot(ys[0][0], ws[0][...], preferred_element_type=F32)
    for k in range(1, n_parts):
        y = y + jnp.dot(ys[k][0], ws[k][...], preferred_element_type=F32)
    x = x_ref[0] + g1_ref[0] * y
    xo_ref[0] = x
    h2 = _norm_mod(x, nw_ref[...], sc_ref[0], sh_ref[0])
    h2_ref[0] = h2
    h_hi = h2.astype(BF16)
    h_lo = (h2 - h_hi.astype(F32)).astype(BF16)
    both = jnp.dot(h_hi, wr_ref[...], preferred_element_type=F32)
    logits = (jnp.dot(h_lo, wr_ref[:, :LANES], preferred_element_type=F32) + both[:, LANES:]) + both[:, :LANES]
    e_out, w_out, cnt = _route(logits)
    re_ref[0] = e_out
    rw_ref[0] = w_out

    @pl.when(jnp.logical_and(pl.program_id(0) == 0, pl.program_id(1) == 0))
    def _():
        cnt_ref[...] = jnp.zeros_like(cnt_ref)

    cnt_ref[...] += cnt


def _mix_out(parts, weights, x, g1, nw, sc, sh, w_router, tm):
    b, l, d = x.shape
    n_parts = len(parts)
    per_batch = g1.shape[0] > 1
    mod_map = (lambda i, j: (i, 0, 0)) if per_batch else (lambda i, j: (0, 0, 0))
    row = lambda i, j: (i, j, 0)
    const2 = lambda i, j: (0, 0)
    in_specs = ([pl.BlockSpec((1, tm, p.shape[2]), row) for p in parts]
                + [pl.BlockSpec(w.shape, const2) for w in weights]
                + [pl.BlockSpec((1, tm, d), row), pl.BlockSpec((1, 1, d), mod_map),
                   pl.BlockSpec((1, d), const2), pl.BlockSpec((1, 1, d), mod_map),
                   pl.BlockSpec((1, 1, d), mod_map), pl.BlockSpec((d, 2 * LANES), const2)])
    return pl.pallas_call(
        functools.partial(_mix_out_kernel, n_parts=n_parts), grid=(b, l // tm),
        in_specs=in_specs,
        out_specs=[pl.BlockSpec((1, tm, d), row), pl.BlockSpec((1, tm, d), row),
                   pl.BlockSpec((1, tm, LANES), row), pl.BlockSpec((1, tm, LANES), row),
                   pl.BlockSpec((1, LANES), const2)],
        out_shape=[jax.ShapeDtypeStruct((b, l, d), F32), jax.ShapeDtypeStruct((b, l, d), F32),
                   jax.ShapeDtypeStruct((b, l, LANES), jnp.int32),
                   jax.ShapeDtypeStruct((b, l, LANES), F32),
                   jax.ShapeDtypeStruct((1, LANES), F32)],
        compiler_params=_cparams(("arbitrary", "arbitrary")),
        name="mix_out")(*parts, *weights, x, g1, nw, sc, sh, w_router)


def _block_layout(counts, n_blocks):
    counts = counts.astype(jnp.int32)
    padded = (counts + MOE_BLOCK - 1) // MOE_BLOCK * MOE_BLOCK
    pad_ends = jnp.cumsum(padded)
    pad_starts = pad_ends - padded
    blk_start = jnp.arange(n_blocks, dtype=jnp.int32) * MOE_BLOCK
    blk_expert = jnp.minimum(jnp.sum((blk_start[:, None] >= pad_ends[None, :]).astype(jnp.int32), axis=1),
                             N_EXPERTS - 1).astype(jnp.int32)
    blk_valid = (blk_start < pad_ends[-1]).astype(jnp.int32)
    return pad_starts, blk_expert, blk_valid


SLOT_TILE = 512


def _slot_kernel(re_ref, base_ref, slot_ref, carry):
    @pl.when(pl.program_id(0) == 0)
    def _():
        carry[...] = jnp.zeros_like(carry)

    e = re_ref[...]
    r = e.shape[0]
    lane = lax.broadcasted_iota(jnp.int32, e.shape, 1)
    oh0 = lane == e[:, 0:1]
    oh1 = lane == e[:, 1:2]
    both = jnp.where(jnp.logical_or(oh0, oh1), 1.0, 0.0)
    strict = (lax.broadcasted_iota(jnp.int32, (r, r), 0) > lax.broadcasted_iota(jnp.int32, (r, r), 1))
    before = jnp.dot(jnp.where(strict, 1.0, 0.0).astype(BF16), both.astype(BF16),
                     preferred_element_type=F32)
    pos = before + carry[...] + base_ref[...]
    s0 = jnp.sum(jnp.where(oh0, pos, 0.0), axis=-1, keepdims=True)
    s1 = jnp.sum(jnp.where(oh1, pos, 0.0), axis=-1, keepdims=True)
    slot_ref[...] = jnp.where(lane == 0, s0, jnp.where(lane == 1, s1, 0.0)).astype(jnp.int32)
    carry[...] += jnp.sum(both, axis=0, keepdims=True)


def _slots(re, base):
    t = re.shape[0]
    r = _row_tile(t, SLOT_TILE)
    return pl.pallas_call(
        _slot_kernel, grid=(t // r,),
        in_specs=[pl.BlockSpec((r, LANES), lambda i: (i, 0)), pl.BlockSpec((1, LANES), lambda i: (0, 0))],
        out_specs=pl.BlockSpec((r, LANES), lambda i: (i, 0)),
        out_shape=jax.ShapeDtypeStruct((t, LANES), jnp.int32),
        scratch_shapes=[pltpu.VMEM((1, LANES), F32)],
        compiler_params=_cparams(("arbitrary",)),
        name="slots")(re, base)


SCATTER_TILE = 256
SLAB = 8


def _scatter_kernel(idx_ref, h_ref, xs_in, xs_out, stage, idx_smem, ssem, isem, *, rt):
    del xs_in
    i = pl.program_id(0)
    n = pl.num_programs(0)
    slot = i % 2

    def drain(s):
        def body(r, carry):
            pltpu.make_async_copy(stage.at[s, pl.ds(0, SLAB)], xs_out.at[pl.ds(0, SLAB)], ssem.at[s]).wait()
            return carry
        lax.fori_loop(0, 2 * rt, body, 0, unroll=8)

    @pl.when(i > 1)
    def _():
        drain(slot)

    cp = pltpu.make_async_copy(idx_ref.at[0], idx_smem.at[pl.ds(slot, 1)], isem.at[0])
    cp.start()
    for s in range(SLAB):
        stage[slot, pl.ds(s, rt, stride=SLAB), :] = h_ref[:, s * LANES:(s + 1) * LANES]
    cp.wait()

    for r in range(rt):
        src = stage.at[slot, pl.ds(r * SLAB, SLAB)]
        d0 = pl.multiple_of(idx_smem[slot, r] * SLAB, SLAB)
        d1 = pl.multiple_of(idx_smem[slot, rt + r] * SLAB, SLAB)
        pltpu.make_async_copy(src, xs_out.at[pl.ds(d0, SLAB)], ssem.at[slot]).start()
        pltpu.make_async_copy(src, xs_out.at[pl.ds(d1, SLAB)], ssem.at[slot]).start()

    @pl.when(i == n - 1)
    def _():
        drain(slot)

        @pl.when(n > 1)
        def _():
            drain(1 - slot)


def _scatter_rows(h2, slot_of, xs):
    t, d = h2.shape
    assert d == SLAB * LANES
    rt = _row_tile(t, SCATTER_TILE)
    nt = t // rt
    idx = jnp.concatenate([slot_of[:, 0].reshape(nt, 1, rt), slot_of[:, 1].reshape(nt, 1, rt)], axis=2)
    return pl.pallas_call(
        functools.partial(_scatter_kernel, rt=rt), grid=(nt,),
        in_specs=[pl.BlockSpec((1, 1, 2 * rt), lambda i: (i, 0, 0)),
                  pl.BlockSpec((rt, d), lambda i: (i, 0)), pl.BlockSpec(memory_space=pl.ANY)],
        out_specs=pl.BlockSpec(memory_space=pl.ANY),
        out_shape=jax.ShapeDtypeStruct(xs.shape, xs.dtype),
        input_output_aliases={2: 0},
        scratch_shapes=[pltpu.VMEM((2, rt * SLAB, LANES), F32), pltpu.SMEM((2, 2 * rt), jnp.int32),
                        pltpu.SemaphoreType.DMA((2,)), pltpu.SemaphoreType.DMA((1,))],
        compiler_params=_cparams(("arbitrary",)),
        name="scatter_rows")(idx, h2, xs)


def _gather_rows(idx_vmem_ref, idx_smem, slot, src_hbm, dst_buf, isem, gsem, n_rows):
    cp = pltpu.make_async_copy(idx_vmem_ref.at[0], idx_smem.at[pl.ds(slot, 1)], isem.at[0])
    cp.start()
    cp.wait()

    for r in range(n_rows):
        tok = pl.multiple_of(idx_smem[slot, r] * SLAB, SLAB)
        pltpu.make_async_copy(src_hbm.at[pl.ds(tok, SLAB)], dst_buf.at[slot, pl.ds(r * SLAB, SLAB)],
                              gsem.at[slot]).start()


def _wait_rows(src_hbm, dst_buf, slot, gsem, n_rows):
    def body(r, carry):
        pltpu.make_async_copy(src_hbm.at[pl.ds(0, SLAB)], dst_buf.at[slot, pl.ds(0, SLAB)],
                              gsem.at[slot]).wait()
        return carry
    lax.fori_loop(0, n_rows, body, 0, unroll=8)


def _rows_from_slabs(ref, lead, first, n):
    return jnp.concatenate([ref[(*lead, pl.ds(first * SLAB + s, n, stride=SLAB), slice(None))]
                            for s in range(SLAB)], axis=1)


def _expert_kernel(be_ref, bv_ref, x_ref, wg_ref, wu_ref, wd_ref, out_ref):
    valid = bv_ref[pl.program_id(0)] > 0

    @pl.when(valid)
    def _():
        x = _rows_from_slabs(x_ref, (), 0, MOE_BLOCK).astype(BF16)
        g = jnp.dot(x, wg_ref[0].astype(BF16), preferred_element_type=F32)
        u = jnp.dot(x, wu_ref[0].astype(BF16), preferred_element_type=F32)
        h = (_silu(g) * u).astype(BF16)
        y = jnp.dot(h, wd_ref[0].astype(BF16), preferred_element_type=F32)
        for s in range(SLAB):
            out_ref[pl.ds(s, MOE_BLOCK, stride=SLAB), :] = y[:, s * LANES:(s + 1) * LANES]

    @pl.when(jnp.logical_not(valid))
    def _():
        out_ref[...] = jnp.zeros_like(out_ref)


def _experts(xs, blk_expert, blk_valid, w_gate, w_up, w_down):
    d = SLAB * LANES
    n_blocks = xs.shape[0] // (MOE_BLOCK * SLAB)
    blk = pl.BlockSpec((MOE_BLOCK * SLAB, LANES), lambda i, be, bv: (i, 0))
    grid_spec = pltpu.PrefetchScalarGridSpec(
        num_scalar_prefetch=2, grid=(n_blocks,),
        in_specs=[blk,
                  pl.BlockSpec((1, d, D_EXPERT), lambda i, be, bv: (be[i], 0, 0)),
                  pl.BlockSpec((1, d, D_EXPERT), lambda i, be, bv: (be[i], 0, 0)),
                  pl.BlockSpec((1, D_EXPERT, d), lambda i, be, bv: (be[i], 0, 0))],
        out_specs=blk)
    return pl.pallas_call(
        _expert_kernel, grid_spec=grid_spec,
        out_shape=jax.ShapeDtypeStruct(xs.shape, F32),
        compiler_params=_cparams(("arbitrary",)),
        name="experts")(blk_expert, blk_valid, xs, w_gate, w_up, w_down)


def _combine_kernel(idx_ref, idx_nxt_ref, y_hbm, x_ref, g2_ref, rw_ref, fn_ref, out_ref,
                    rbuf, idx_smem, gsem, isem, *, tm, final_norm):
    i = pl.program_id(0)
    n = pl.num_programs(0)
    slot = i % 2

    @pl.when(i == 0)
    def _():
        _gather_rows(idx_ref, idx_smem, 0, y_hbm, rbuf, isem, gsem, 2 * tm)

    @pl.when(i + 1 < n)
    def _():
        _gather_rows(idx_nxt_ref, idx_smem, 1 - slot, y_hbm, rbuf, isem, gsem, 2 * tm)

    _wait_rows(y_hbm, rbuf, slot, gsem, 2 * tm)
    rw = rw_ref[...]
    y = (rw[:, 0:1] * _rows_from_slabs(rbuf, (slot,), 0, tm)
         + rw[:, 1:2] * _rows_from_slabs(rbuf, (slot,), tm, tm))
    x = x_ref[...] + g2_ref[0] * y
    if final_norm:
        ms = jnp.mean(x * x, axis=-1, keepdims=True)
        x = (x * lax.rsqrt(ms + RMS_EPS)) * fn_ref[...]
    out_ref[...] = x


def _combine(ybuf, slot_of, x, g2, rw, fn, rows_per_batch, tm, final_norm):
    t, d = x.shape
    nt = t // tm
    idx = jnp.concatenate([slot_of[:, 0].reshape(nt, 1, tm), slot_of[:, 1].reshape(nt, 1, tm)], axis=2)
    per_batch = g2.shape[0] > 1
    tpb = rows_per_batch // tm
    mod_map = (lambda i: (i // tpb, 0, 0)) if per_batch else (lambda i: (0, 0, 0))
    return pl.pallas_call(
        functools.partial(_combine_kernel, tm=tm, final_norm=final_norm), grid=(nt,),
        in_specs=[pl.BlockSpec((1, 1, 2 * tm), lambda i: (i, 0, 0)),
                  pl.BlockSpec((1, 1, 2 * tm), lambda i: (jnp.minimum(i + 1, nt - 1), 0, 0)),
                  pl.BlockSpec(memory_space=pl.ANY),
                  pl.BlockSpec((tm, d), lambda i: (i, 0)),
                  pl.BlockSpec((1, 1, d), mod_map),
                  pl.BlockSpec((tm, LANES), lambda i: (i, 0)),
                  pl.BlockSpec((1, d), lambda i: (0, 0))],
        out_specs=pl.BlockSpec((tm, d), lambda i: (i, 0)),
        out_shape=jax.ShapeDtypeStruct((t, d), F32),
        scratch_shapes=[pltpu.VMEM((2, 2 * tm * SLAB, LANES), F32),
                        pltpu.SMEM((2, 2 * tm), jnp.int32),
                        pltpu.SemaphoreType.DMA((2,)),
                        pltpu.SemaphoreType.DMA((1,))],
        compiler_params=_cparams(("arbitrary",)),
        name="combine")(idx, idx, ybuf, x, g2, rw, fn)


def _proj1_kernel(x_ref, nw_ref, sc_ref, sh_ref, w_ref, qn_ref, kn_ref, cos_ref, sin_ref,
                  *out_refs, rope, want_q):
    h = _norm_mod(x_ref[0], nw_ref[...], sc_ref[0], sh_ref[0]).astype(BF16)
    hd = ATTN_HEAD_DIM
    nq = ATTN_HEADS * hd
    nkv = ATTN_KV_HEADS * hd

    def head_norm(xh, gain):
        ms = jnp.mean(xh * xh, axis=-1, keepdims=True)
        xh = (xh * lax.rsqrt(ms + RMS_EPS)) * gain
        if rope:
            xh = xh * cos_ref[...] + pltpu.roll(xh, hd // 2, 1) * sin_ref[...]
        return xh

    if want_q:
        q_ref, k_ref, v_ref = out_refs
        qf = jnp.dot(h, w_ref[:, 0:nq], preferred_element_type=F32)
        for i in range(ATTN_HEADS):
            xh = head_norm(qf[:, i * hd:(i + 1) * hd], qn_ref[...])
            q_ref[0, :, i * hd:(i + 1) * hd] = (xh * (hd ** -0.5)).astype(BF16)
    else:
        k_ref, v_ref = out_refs
    kf = jnp.dot(h, w_ref[:, nq:nq + nkv], preferred_element_type=F32)
    for i in range(ATTN_KV_HEADS):
        k_ref[0, :, i * hd:(i + 1) * hd] = head_norm(kf[:, i * hd:(i + 1) * hd], kn_ref[...]).astype(BF16)
    v_ref[0] = jnp.dot(h, w_ref[:, nq + nkv:], preferred_element_type=F32).astype(BF16)


def _proj1(x, nw, sc, sh, w, qn, kn, cos_t, sin_t, tm, rope, want_q):
    b, l, d = x.shape
    per_batch = sc.shape[0] > 1
    mod_map = (lambda i, j: (i, 0, 0)) if per_batch else (lambda i, j: (0, 0, 0))
    row = lambda i, j: (i, j, 0)
    const2 = lambda i, j: (0, 0)
    hd = ATTN_HEAD_DIM
    nq, nkv = ATTN_HEADS * hd, ATTN_KV_HEADS * hd
    out_specs = [pl.BlockSpec((1, tm, nkv), row), pl.BlockSpec((1, tm, nkv), row)]
    out_shape = [jax.ShapeDtypeStruct((b, l, nkv), BF16), jax.ShapeDtypeStruct((b, l, nkv), BF16)]
    if want_q:
        out_specs = [pl.BlockSpec((1, tm, nq), row)] + out_specs
        out_shape = [jax.ShapeDtypeStruct((b, l, nq), BF16)] + out_shape
    return pl.pallas_call(
        functools.partial(_proj1_kernel, rope=rope, want_q=want_q), grid=(b, l // tm),
        in_specs=[pl.BlockSpec((1, tm, d), row), pl.BlockSpec((1, d), const2),
                  pl.BlockSpec((1, 1, d), mod_map), pl.BlockSpec((1, 1, d), mod_map),
                  pl.BlockSpec(w.shape, const2), pl.BlockSpec((1, hd), const2),
                  pl.BlockSpec((1, hd), const2),
                  pl.BlockSpec((tm, hd), lambda i, j: (j, 0)), pl.BlockSpec((tm, hd), lambda i, j: (j, 0))],
        out_specs=out_specs, out_shape=out_shape,
        compiler_params=_cparams(("arbitrary", "arbitrary")),
        name="proj1")(x, nw, sc, sh, w, qn, kn, cos_t, sin_t)


def _attn_kernel(q_ref, kl_ref, vl_ref, kc_ref, vc_ref, o_ref):
    hd = ATTN_HEAD_DIM
    k_l, v_l, k_c, v_c = kl_ref[0], vl_ref[0], kc_ref[0], vc_ref[0]
    nt = (((1,), (1,)), ((), ()))
    for g in range(ATTN_GROUP):
        qh = q_ref[0, :, g * hd:(g + 1) * hd]
        s_l = lax.dot_general(qh, k_l, nt, preferred_element_type=F32)
        s_c = lax.dot_general(qh, k_c, nt, preferred_element_type=F32)
        m = jnp.maximum(jnp.max(s_l, axis=-1, keepdims=True), jnp.max(s_c, axis=-1, keepdims=True))
        p_l = jnp.exp(s_l - m)
        p_c = jnp.exp(s_c - m)
        den = jnp.sum(p_l, axis=-1, keepdims=True) + jnp.sum(p_c, axis=-1, keepdims=True)
        o = (jnp.dot(p_l.astype(BF16), v_l, preferred_element_type=F32)
             + jnp.dot(p_c.astype(BF16), v_c, preferred_element_type=F32))
        o_ref[0, :, g * hd:(g + 1) * hd] = (o / den).astype(BF16)


def _attention(q, k_l, v_l, k_c, v_c, tq):
    b, l, nq = q.shape
    n_ctx = k_c.shape[1]
    hd = ATTN_HEAD_DIM
    gw = ATTN_GROUP * hd
    return pl.pallas_call(
        _attn_kernel, grid=(b, ATTN_KV_HEADS, l // tq),
        in_specs=[pl.BlockSpec((1, tq, gw), lambda i, kv, j: (i, j, kv)),
                  pl.BlockSpec((1, l, hd), lambda i, kv, j: (i, 0, kv)),
                  pl.BlockSpec((1, l, hd), lambda i, kv, j: (i, 0, kv)),
                  pl.BlockSpec((1, n_ctx, hd), lambda i, kv, j: (i, 0, kv)),
                  pl.BlockSpec((1, n_ctx, hd), lambda i, kv, j: (i, 0, kv))],
        out_specs=pl.BlockSpec((1, tq, gw), lambda i, kv, j: (i, j, kv)),
        out_shape=jax.ShapeDtypeStruct((b, l, nq), BF16),
        compiler_params=_cparams(("arbitrary", "arbitrary", "arbitrary")),
        name="attention")(q, k_l, v_l, k_c, v_c)


def _row_tile(n, pref):
    return pref if n % pref == 0 else n


def _moe(streams, w_gate, w_up, w_down):
    d = streams[0][0].shape[1]
    t_all = sum(s[0].shape[0] for s in streams)
    n_blocks = -(-2 * t_all // MOE_BLOCK) + N_EXPERTS
    counts = sum(s[2] for s in streams)[0, :N_EXPERTS]
    pad_starts, blk_expert, blk_valid = _block_layout(counts, n_blocks)
    base = jnp.pad(pad_starts.astype(F32), (0, LANES - N_EXPERTS)).reshape(1, LANES)
    xs = jnp.zeros((n_blocks * MOE_BLOCK * SLAB, d // SLAB), F32)
    slot_tabs = []
    for h2, re, cnt in streams:
        slot_tab = _slots(re, base)
        xs = _scatter_rows(h2, slot_tab[:, :2], xs)
        slot_tabs.append(slot_tab)
        base = base + cnt
    ybuf = _experts(xs, blk_expert, blk_valid, w_gate, w_up, w_down)
    return ybuf, slot_tabs


def kernel(x, c, ctx, c_ctx, mod_w, mod_b, norm1, norm2, ab_w_in, ab_conv_a, ab_conv_qkv, ab_a_log,
           ab_dt_bias, ab_gnorm, ab_w_out, attn_w_qkv, attn_q_norm, attn_k_norm, attn_w_o, moe_w_group,
           moe_w_expert, moe_w_gate, moe_w_up, moe_w_down, final_norm):
    b, l, d = x.shape
    n_ctx = ctx.shape[1]
    assert mod_w.shape[0] == 2 and d == D_MODEL
    assert l % PRE_TILE == 0 and n_ctx % PRE_TILE == 0 and (l + n_ctx) % (SUPER * P1_UNITS) == 0
    tm_l = _row_tile(l, 512)
    tm_c = _row_tile(n_ctx, 256)

    rpad = (-(b + 1)) % 8
    cond = jnp.concatenate([c, c_ctx[None, :], jnp.zeros((rpad, d), F32)], axis=0)
    mod = _modulation(cond, mod_w, mod_b)

    def mods(i):
        lat = [mod[i, :b, k * d:(k + 1) * d].reshape(b, 1, d) for k in range(6)]
        cx = [mod[i, b:b + 1, k * d:(k + 1) * d].reshape(1, 1, d) for k in range(6)]
        return lat, cx

    def router_w(i):
        w = jnp.concatenate([moe_w_group[i], moe_w_expert[i]], axis=1)
        w = jnp.pad(w, ((0, 0), (0, LANES - w.shape[1])))
        w_hi = w.astype(BF16)
        w_lo = (w - w_hi.astype(F32)).astype(BF16)
        return jnp.concatenate([w_hi, w_lo], axis=1)

    (sh1, sc1, g1, sh2, sc2, g2), (csh1, csc1, cg1, csh2, csc2, cg2) = mods(0)
    w_in = ab_w_in[0]
    n_gate = 4 * GDN_HEADS
    w0 = jnp.concatenate([w_in[:, :P0_A + QKV_WIDTH + P0_Z], w_in[:, -n_gate:],
                          jnp.zeros((d, LANES - n_gate), F32)], axis=1).astype(BF16)
    nw1 = norm1[0].reshape(1, d)
    a_l, qkv_l, z_l, gt_l = _proj0(x, nw1, sc1, sh1, w0, tm_l)
    a_c, qkv_c, z_c, gt_c = _proj0(ctx, nw1, csc1, csh1, w0, tm_c)

    def pair_gates(gt):
        n = gt.shape[1]
        g4 = gt[:, :, :n_gate].reshape(b, n, 4, GDN_PAIRS, 2)
        return jnp.transpose(g4, (0, 3, 1, 2, 4)).reshape(b, GDN_PAIRS, n, 8)

    def pair_vec(v):
        v2 = v.reshape(2, GDN_PAIRS, 2)
        z2 = jnp.zeros_like(v2)
        return jnp.stack([v2[0], z2[0], v2[1], z2[1]], axis=1).reshape(GDN_PAIRS, 1, 8)

    gnorm2 = jnp.tile(ab_gnorm[0], 2).reshape(1, LANES)
    ya_c, yb_c, ya_l, yb_l = _gdn_mixer(
        a_c, qkv_c, z_c, pair_gates(gt_c), a_l, qkv_l, z_l, pair_gates(gt_l),
        ab_conv_qkv[0], ab_conv_a[0], pair_vec(ab_a_log[0]), pair_vec(ab_dt_bias[0]), gnorm2)

    w_out = ab_w_out[0].astype(BF16)
    w_out_a, w_out_b = w_out[:A_WIDTH], w_out[A_WIDTH:]
    nw2 = norm2[0].reshape(1, d)
    wr0 = router_w(0)
    x1, h2_l, re_l, rw_l, cnt_l = _mix_out([ya_l, yb_l], [w_out_a, w_out_b], x, g1, nw2, sc2, sh2, wr0, tm_l)
    c1, h2_c, re_c, rw_c, cnt_c = _mix_out([ya_c, yb_c], [w_out_a, w_out_b], ctx, cg1, nw2, csc2, csh2, wr0,
                                           tm_c)

    t_l, t_c = b * l, b * n_ctx
    ybuf, (slot_l, slot_c) = _moe([(h2_l.reshape(t_l, d), re_l.reshape(t_l, LANES), cnt_l),
                                   (h2_c.reshape(t_c, d), re_c.reshape(t_c, LANES), cnt_c)],
                                  moe_w_gate[0], moe_w_up[0], moe_w_down[0])
    fn = final_norm.reshape(1, d)
    x2 = _combine(ybuf, slot_l, x1.reshape(t_l, d), g2, rw_l.reshape(t_l, LANES), fn,
                  l, _row_tile(l, 256), False).reshape(b, l, d)
    ctx2 = _combine(ybuf, slot_c, c1.reshape(t_c, d), cg2, rw_c.reshape(t_c, LANES), fn,
                    n_ctx, _row_tile(n_ctx, 256), False).reshape(b, n_ctx, d)

    (sh1, sc1, g1, sh2, sc2, g2), (csh1, csc1, cg1, csh2, csc2, cg2) = mods(1)
    hd = ATTN_HEAD_DIM
    perm = jnp.concatenate([jnp.arange(0, hd, 2), jnp.arange(1, hd, 2)])
    n_qk = (ATTN_HEADS + ATTN_KV_HEADS) * hd
    col_perm = (jnp.arange(n_qk).reshape(-1, hd)[:, perm]).reshape(-1)
    w_qkv = attn_w_qkv[0]
    w1 = jnp.concatenate([w_qkv[:, col_perm], w_qkv[:, n_qk:]], axis=1).astype(BF16)
    qn = attn_q_norm[0][perm].reshape(1, hd)
    kn = attn_k_norm[0][perm].reshape(1, hd)
    pos = jnp.arange(l, dtype=jnp.int32)
    n_freq = hd // 4
    inv = ROPE_THETA ** (-jnp.arange(n_freq, dtype=F32) / n_freq)
    ang = jnp.concatenate([(pos // GRID_W).astype(F32)[:, None] * inv,
                           (pos % GRID_W).astype(F32)[:, None] * inv], axis=-1)
    cos_t = jnp.concatenate([jnp.cos(ang), jnp.cos(ang)], axis=-1)
    sin_t = jnp.concatenate([-jnp.sin(ang), jnp.sin(ang)], axis=-1)
    nw1 = norm1[1].reshape(1, d)
    q_l, k_l, v_l = _proj1(x2, nw1, sc1, sh1, w1, qn, kn, cos_t, sin_t, tm_l, True, True)
    dummy = jnp.zeros((n_ctx, hd), F32)
    k_c, v_c = _proj1(ctx2, nw1, csc1, csh1, w1, qn, kn, dummy, dummy, tm_c, False, False)
    o = _attention(q_l, k_l, v_l, k_c, v_c, _row_tile(l, 256))

    nw2 = norm2[1].reshape(1, d)
    x3, h2_l, re_l, rw_l, cnt_l = _mix_out([o], [attn_w_o[0].astype(BF16)], x2, g1, nw2, sc2, sh2,
                                           router_w(1), tm_l)
    ybuf, (slot_l,) = _moe([(h2_l.reshape(t_l, d), re_l.reshape(t_l, LANES), cnt_l)],
                           moe_w_gate[1], moe_w_up[1], moe_w_down[1])
    out = _combine(ybuf, slot_l, x3.reshape(t_l, d), g2, rw_l.reshape(t_l, LANES), fn,
                   l, _row_tile(l, 256), True)
    return out.reshape(b, l, d)
```

```python
import functools

import jax
import jax.numpy as jnp
from jax import lax
from jax.experimental import pallas as pl
from jax.experimental.pallas import tpu as pltpu

F32 = jnp.float32
BF16 = jnp.bfloat16
HI = lax.Precision.HIGHEST

D_MODEL = 1024
RMS_EPS = 1e-6
L2_EPS = 1e-6
GRID_W = 64
A_WIDTH = 512
GDN_HEADS = 8
GDN_DK = 64
GDN_DV = 64
GDN_CHUNK = 64
QKV_WIDTH = 1536
ATTN_HEADS = 8
ATTN_KV_HEADS = 2
ATTN_GROUP = 4
ATTN_HEAD_DIM = 128
ROPE_THETA = 10000.0
N_GROUPS = 8
EXPERTS_PER_GROUP = 8
N_EXPERTS = 64
D_EXPERT = 384
MOE_BLOCK = 256

LANES = 128
GDN_PAIRS = GDN_HEADS // 2
SUPER = 2 * GDN_CHUNK
VMEM_LIMIT = 52 * 1024 * 1024


def _cparams(sem):
    return pltpu.CompilerParams(dimension_semantics=sem, vmem_limit_bytes=VMEM_LIMIT)


def _silu(x):
    return x * jax.nn.sigmoid(x)


def _norm_mod(x, nw, sc, sh):
    ms = jnp.mean(x * x, axis=-1, keepdims=True)
    return (x * lax.rsqrt(ms + RMS_EPS)) * nw * (1.0 + sc) + sh


def _bdot(a, b):
    return jnp.dot(a.astype(BF16), b.astype(BF16), preferred_element_type=F32)


def _bdot_nt(a, b):
    return lax.dot_general(a.astype(BF16), b.astype(BF16), (((1,), (1,)), ((), ())),
                           preferred_element_type=F32)


def _bdot_tn(a, b):
    return lax.dot_general(a.astype(BF16), b.astype(BF16), (((0,), (0,)), ((), ())),
                           preferred_element_type=F32)


def _mod_kernel(c_ref, w_ref, b_ref, o_ref):
    c = c_ref[...]
    o_ref[0] = jnp.dot(_silu(c), w_ref[0], precision=HI, preferred_element_type=F32) + b_ref[0]


def _modulation(cond, mod_w, mod_b):
    depth, d, n = mod_w.shape
    r = cond.shape[0]
    tn = 512
    return pl.pallas_call(
        _mod_kernel, grid=(depth, n // tn),
        in_specs=[pl.BlockSpec((r, d), lambda i, j: (0, 0)),
                  pl.BlockSpec((1, d, tn), lambda i, j: (i, 0, j)),
                  pl.BlockSpec((1, 1, tn), lambda i, j: (i, 0, j))],
        out_specs=pl.BlockSpec((1, r, tn), lambda i, j: (i, 0, j)),
        out_shape=jax.ShapeDtypeStruct((depth, r, n), F32),
        compiler_params=_cparams(("arbitrary", "arbitrary")),
        name="modulation")(cond, mod_w, mod_b.reshape(depth, 1, n))


P0_A = 3 * A_WIDTH
P0_Z = GDN_HEADS * GDN_DV
P0_COLS = P0_A + QKV_WIDTH + P0_Z + LANES


def _proj0_kernel(x_ref, nw_ref, sc_ref, sh_ref, w_ref, a_ref, qkv_ref, z_ref, g_ref):
    h = _norm_mod(x_ref[0], nw_ref[...], sc_ref[0], sh_ref[0]).astype(BF16)
    c0, c1, c2 = P0_A, P0_A + QKV_WIDTH, P0_A + QKV_WIDTH + P0_Z
    a_ref[0] = jnp.dot(h, w_ref[:, 0:c0], preferred_element_type=F32).astype(BF16)
    qkv_ref[0] = jnp.dot(h, w_ref[:, c0:c1], preferred_element_type=F32).astype(BF16)
    z_ref[0] = jnp.dot(h, w_ref[:, c1:c2], preferred_element_type=F32).astype(BF16)
    g_ref[0] = jnp.dot(h, w_ref[:, c2:], preferred_element_type=F32)


def _proj0(x, nw, sc, sh, w, tm):
    b, l, d = x.shape
    per_batch = sc.shape[0] > 1
    mod_map = (lambda i, j: (i, 0, 0)) if per_batch else (lambda i, j: (0, 0, 0))
    row = lambda i, j: (i, j, 0)
    return pl.pallas_call(
        _proj0_kernel, grid=(b, l // tm),
        in_specs=[pl.BlockSpec((1, tm, d), row),
                  pl.BlockSpec((1, d), lambda i, j: (0, 0)),
                  pl.BlockSpec((1, 1, d), mod_map),
                  pl.BlockSpec((1, 1, d), mod_map),
                  pl.BlockSpec((d, P0_COLS), lambda i, j: (0, 0))],
        out_specs=[pl.BlockSpec((1, tm, P0_A), row),
                   pl.BlockSpec((1, tm, QKV_WIDTH), row),
                   pl.BlockSpec((1, tm, P0_Z), row),
                   pl.BlockSpec((1, tm, LANES), row)],
        out_shape=[jax.ShapeDtypeStruct((b, l, P0_A), BF16),
                   jax.ShapeDtypeStruct((b, l, QKV_WIDTH), BF16),
                   jax.ShapeDtypeStruct((b, l, P0_Z), BF16),
                   jax.ShapeDtypeStruct((b, l, LANES), F32)],
        compiler_params=_cparams(("arbitrary", "arbitrary")),
        name="proj0")(x, nw, sc, sh, w)


PRE_TILE = 256
P1_UNITS = 6


def _gdn_kernel(qc_ref, kc_ref, vc_ref, ql_ref, kl_ref, vl_ref,
                xac_ref, gbc_ref, gcc_ref, xal_ref, gbl_ref, gcl_ref,
                zc_ref, zl_ref, gtc_ref, gtl_ref,
                cwq_ref, cwk_ref, cwv_ref, cwa_ref, a8_ref, dt8_ref, gn_ref,
                yac_ref, ybc_ref, yal_ref, ybl_ref,
                qn_s, kn_s, vn_s, g_s, bt_s, m_s, qq_s, qe_s, ou_s, gl_s, o_s,
                *, n_ctx, n_lat):
    n_tot = n_ctx + n_lat
    lane = lax.broadcasted_iota(jnp.int32, (1, LANES), 1)
    lo = lane < GDN_DK
    lo2 = (lax.broadcasted_iota(jnp.int32, (1, 2 * LANES), 1) & (LANES - 1)) < GDN_DK
    head_blk = (lax.broadcasted_iota(jnp.int32, (LANES, LANES), 0) // GDN_DK
                == lax.broadcasted_iota(jnp.int32, (LANES, LANES), 1) // GDN_DK)
    blk_ones = jnp.where(head_blk, 1.0, 0.0).astype(BF16)

    def conv3_tile(ref, r0, n, w_ref, mul_ref=None):
        def rows(start, size):
            v = ref[0, pl.ds(start, size), :].astype(F32)
            if mul_ref is not None:
                v = v * mul_ref[0, pl.ds(start, size), :].astype(F32)
            return v
        cur = rows(r0, PRE_TILE)
        prev = rows(pl.multiple_of(jnp.maximum(r0 - 16, 0), 16), 16)[15:16, :] * (r0 > 0).astype(F32)
        nxt = (rows(pl.multiple_of(jnp.minimum(r0 + PRE_TILE, n - 16), 16), 16)[0:1, :]
               * (r0 + PRE_TILE < n).astype(F32))
        row = lax.broadcasted_iota(jnp.int32, (PRE_TILE, 1), 0)
        xm = jnp.where(row == 0, prev, pltpu.roll(cur, 1, 0))
        xp = jnp.where(row == PRE_TILE - 1, nxt, pltpu.roll(cur, PRE_TILE - 1, 0))
        return xm * w_ref[0:1, :] + cur * w_ref[1:2, :] + xp * w_ref[2:3, :]

    def head_sumsq(x):
        x2 = x * x
        hi = x2.astype(BF16)
        lo = (x2 - hi.astype(F32)).astype(BF16)
        return (jnp.dot(lo, blk_ones, preferred_element_type=F32)
                + jnp.dot(hi, blk_ones, preferred_element_type=F32))

    def prepass(seg_off, n, q_ref, k_ref, v_ref, gt_ref):
        def body(t, carry):
            r0 = pl.multiple_of(t * PRE_TILE, PRE_TILE)
            dst = pl.ds(pl.multiple_of(seg_off + r0, PRE_TILE), PRE_TILE)
            q = _silu(conv3_tile(q_ref, r0, n, cwq_ref))
            k = _silu(conv3_tile(k_ref, r0, n, cwk_ref))
            v = _silu(conv3_tile(v_ref, r0, n, cwv_ref))
            qn_s[dst, :] = q * lax.rsqrt(head_sumsq(q) + L2_EPS) * (GDN_DK ** -0.5)
            kn_s[dst, :] = k * lax.rsqrt(head_sumsq(k) + L2_EPS)
            vn_s[dst, :] = v
            gt = gt_ref[0, 0, pl.ds(r0, PRE_TILE), :]
            sp_in = gt + dt8_ref[0]
            softplus = jnp.maximum(sp_in, 0.0) + jnp.log1p(jnp.exp(-jnp.abs(sp_in)))
            graw = -jnp.exp(a8_ref[0]) * softplus
            beta = jax.nn.sigmoid(gt)
            for d in range(2):
                c = 4 * d
                g_s[d, dst, :] = jnp.where(lo, graw[:, c:c + 1], graw[:, c + 1:c + 2])
                bt_s[d, dst, :] = jnp.where(lo, beta[:, c + 2:c + 3], beta[:, c + 3:c + 4])
            return carry
        lax.fori_loop(0, n // PRE_TILE, body, 0)

    prepass(0, n_ctx, qc_ref, kc_ref, vc_ref, gtc_ref)
    prepass(n_ctx, n_lat, ql_ref, kl_ref, vl_ref, gtl_ref)

    ii = lax.broadcasted_iota(jnp.int32, (SUPER, SUPER), 0)
    jj = lax.broadcasted_iota(jnp.int32, (SUPER, SUPER), 1)
    xr = ii ^ jj
    same_chunk = xr < GDN_CHUNK
    top = lax.broadcasted_iota(jnp.int32, (SUPER, 1), 0) < GDN_CHUNK
    eye = jnp.where(ii == jj, 1.0, 0.0).astype(F32)
    cs_all = jnp.where(same_chunk, 1.0, 0.0).astype(F32)

    chains = ((0, 0), (0, 1), (1, 0), (1, 1))

    zero_b = jnp.zeros((SUPER, SUPER), BF16)

    def setup_unit(r0):
        rows = pl.ds(r0, SUPER)
        q_s = qn_s[rows, :]
        k_s = kn_s[rows, :]
        v_s = vn_s[rows, :]
        kb = k_s.astype(BF16)
        gram, qk = [], []
        for hh in range(2):
            hm = lo if hh == 0 else jnp.logical_not(lo)
            gram.append(_bdot_nt(jnp.where(hm, k_s, 0.0), kb))
            qk.append(_bdot_nt(jnp.where(hm, q_s, 0.0), kb))
        valid = [jnp.logical_and(same_chunk, jj <= ii), jnp.logical_and(same_chunk, jj >= ii)]
        b_l = [bt_s[0, rows, :], bt_s[1, rows, :]]
        sel = jnp.concatenate([jnp.where(valid[0], 1.0, 0.0), jnp.where(valid[1], 1.0, 0.0), cs_all],
                              axis=0).astype(BF16)
        g_rem = jnp.concatenate([g_s[0, rows, :], g_s[1, rows, :]], axis=1)
        parts = []
        for _ in range(3):
            part = g_rem.astype(BF16)
            parts.append(jnp.dot(sel, part, preferred_element_type=F32))
            g_rem = g_rem - part.astype(F32)
        cums = (parts[2] + parts[1]) + parts[0]
        gc = [cums[0:SUPER, 0:LANES], cums[SUPER:2 * SUPER, LANES:]]
        gcl = [cums[2 * SUPER:, 0:LANES], cums[2 * SUPER:, LANES:]]
        u = dict(r0=r0, rows=rows, qg=[], kg=[], rhs=[], nm=[], att=[], t_inv=[])
        gct = []
        for d in range(2):
            egc = jnp.exp(gc[d])
            u['qg'].append(q_s * egc)
            u['kg'].append((k_s * jnp.exp(gcl[d] - gc[d])).astype(BF16))
            gl_s[d, rows, :] = jnp.exp(gcl[d])
            u['rhs'].append(jnp.concatenate([v_s * b_l[d], k_s * b_l[d] * egc], axis=1).astype(BF16))
            gct.append(gc[d].T)
        for d, hh in chains:
            c0 = hh * GDN_DK
            diff = gc[d][:, c0:c0 + 1] - gct[d][c0:c0 + 1, :]
            dec = jnp.exp(jnp.where(valid[d], diff, -jnp.inf))
            u['nm'].append(jnp.where(ii == jj, 0.0, b_l[d][:, c0:c0 + 1] * gram[hh] * dec).astype(BF16))
            u['att'].append((qk[hh] * dec).astype(BF16))
            u['t_inv'].append(eye - jnp.where(xr < 2, u['nm'][-1].astype(F32), 0.0))
        return u

    def finish_unit(u, uw, auw):
        for d in range(2):
            c_a, c_b = 2 * d, 2 * d + 1
            uw_d = jnp.where(lo2, uw[c_a], uw[c_b])
            auw_d = jnp.where(lo2, auw[c_a], auw[c_b])
            ou_s[d, u['rows'], :] = auw_d[:, :LANES]
            qe_s[d, u['rows'], :] = (u['qg'][d] - auw_d[:, LANES:]).astype(BF16)
            uw_b = uw_d.astype(BF16)
            kg = u['kg'][d]
            for half in range(2):
                keep = top if half == 0 else jnp.logical_not(top)
                mq = _bdot_tn(jnp.where(keep, kg, jnp.zeros_like(kg)), uw_b)
                crow = pl.ds(pl.multiple_of(2 * u['r0'] + half * LANES, LANES), LANES)
                qq_s[d, crow, :] = jnp.where(head_blk, mq[:, :LANES], 0.0)
                m_s[d, crow, :] = jnp.where(head_blk, mq[:, LANES:], 0.0).astype(BF16)

    def phase1(s, carry):
        units = [setup_unit(pl.multiple_of((s * P1_UNITS + k) * SUPER, SUPER)) for k in range(P1_UNITS)]
        nm = [m for u in units for m in u['nm']]
        t_inv = [t for u in units for t in u['t_inv']]
        nc = len(nm)
        for lvl in range(1, 6):
            sel = (xr >> lvl) == 1
            tb = [t.astype(BF16) for t in t_inv]
            ot = [jnp.dot(jnp.where(sel, nm[c], zero_b), tb[c], preferred_element_type=F32) for c in range(nc)]
            tot = [jnp.dot(tb[c], ot[c].astype(BF16), preferred_element_type=F32) for c in range(nc)]
            t_inv = [t_inv[c] - tot[c] for c in range(nc)]
        rhs = [u['rhs'][d] for u in units for d, _ in chains]
        att = [a for u in units for a in u['att']]
        uw = [jnp.dot(t_inv[c].astype(BF16), rhs[c], preferred_element_type=F32) for c in range(nc)]
        auw = [jnp.dot(att[c], uw[c].astype(BF16), preferred_element_type=F32) for c in range(nc)]
        for k, u in enumerate(units):
            finish_unit(u, uw[4 * k:4 * k + 4], auw[4 * k:4 * k + 4])
        return carry

    lax.fori_loop(0, n_tot // (SUPER * P1_UNITS), phase1, 0)

    n_ch = n_tot // GDN_CHUNK
    n_cc = n_ctx // GDN_CHUNK

    def scan_step(n, states):
        c_f = n
        c_b = jnp.where(n < n_cc, n_cc - 1 - n, n_ch - 1 + n_cc - n)
        new_states = []
        for d, c in ((0, c_f), (1, c_b)):
            st = states[d]
            rows = pl.ds(pl.multiple_of(c * GDN_CHUNK, GDN_CHUNK), GDN_CHUNK)
            crow = pl.ds(pl.multiple_of(c * LANES, LANES), LANES)
            lhs = jnp.concatenate([m_s[d, crow, :], qe_s[d, rows, :]], axis=0)
            res = jnp.dot(lhs, st.astype(BF16), preferred_element_type=F32)
            o_s[d, rows, :] = res[LANES:] + ou_s[d, rows, :]
            gl = gl_s[d, pl.ds(pl.multiple_of(c * GDN_CHUNK, GDN_CHUNK), 1), :]
            new_states.append(st * gl + qq_s[d, crow, :] - res[:LANES])
        return tuple(new_states)

    s0 = jnp.zeros((LANES, LANES), F32)
    lax.fori_loop(0, n_ch, scan_step, (s0, s0))

    def postpass(seg_off, n, z_ref, xa_ref, gb_ref, gcv_ref, ya_ref, yb_ref):
        def body(t, carry):
            r0 = pl.multiple_of(t * PRE_TILE, PRE_TILE)
            src = pl.ds(pl.multiple_of(seg_off + r0, PRE_TILE), PRE_TILE)
            dst = pl.ds(r0, PRE_TILE)
            o = o_s[0, src, :] + o_s[1, src, :]
            ms = head_sumsq(o) * (1.0 / GDN_DV)
            z = z_ref[0, dst, :].astype(F32)
            yb_ref[0, dst, :] = ((o * lax.rsqrt(ms + RMS_EPS)) * gn_ref[...] * _silu(z)).astype(BF16)
            conv = conv3_tile(gcv_ref, r0, n, cwa_ref, mul_ref=xa_ref)
            ya_ref[0, dst, :] = (gb_ref[0, dst, :].astype(F32) * conv).astype(BF16)
            return carry
        lax.fori_loop(0, n // PRE_TILE, body, 0)

    postpass(0, n_ctx, zc_ref, xac_ref, gbc_ref, gcc_ref, yac_ref, ybc_ref)
    postpass(n_ctx, n_lat, zl_ref, xal_ref, gbl_ref, gcl_ref, yal_ref, ybl_ref)


def _gdn_mixer(a_c, qkv_c, z_c, gt_c, a_l, qkv_l, z_l, gt_l, conv_qkv, conv_a, a8, dt8, gnorm2):
    b, n_ctx, _ = qkv_c.shape
    n_lat = qkv_l.shape[1]
    n_tot = n_ctx + n_lat
    np_ = GDN_PAIRS

    def seq(n, off):
        return pl.BlockSpec((1, n, LANES), lambda i, p, off=off: (i, 0, off + p))

    def cw(off):
        return pl.BlockSpec((3, LANES), lambda i, p, off=off: (0, off + p))

    gate = lambda n: pl.BlockSpec((1, 1, n, 8), lambda i, p: (i, p, 0, 0))
    vec8 = pl.BlockSpec((1, 1, 8), lambda i, p: (p, 0, 0))
    in_specs = [seq(n_ctx, 0), seq(n_ctx, np_), seq(n_ctx, 2 * np_),
                seq(n_lat, 0), seq(n_lat, np_), seq(n_lat, 2 * np_),
                seq(n_ctx, 0), seq(n_ctx, np_), seq(n_ctx, 2 * np_),
                seq(n_lat, 0), seq(n_lat, np_), seq(n_lat, 2 * np_),
                seq(n_ctx, 0), seq(n_lat, 0), gate(n_ctx), gate(n_lat),
                cw(0), cw(np_), cw(2 * np_), cw(0), vec8, vec8,
                pl.BlockSpec((1, LANES), lambda i, p: (0, 0))]
    out_specs = [seq(n_ctx, 0), seq(n_ctx, 0), seq(n_lat, 0), seq(n_lat, 0)]
    out_shape = [jax.ShapeDtypeStruct((b, n_ctx, A_WIDTH), BF16),
                 jax.ShapeDtypeStruct((b, n_ctx, GDN_HEADS * GDN_DV), BF16),
                 jax.ShapeDtypeStruct((b, n_lat, A_WIDTH), BF16),
                 jax.ShapeDtypeStruct((b, n_lat, GDN_HEADS * GDN_DV), BF16)]
    f32buf = lambda *lead: pltpu.VMEM((*lead, n_tot, LANES), F32)
    scratch = [f32buf(), f32buf(), f32buf(), f32buf(2), f32buf(2),
               pltpu.VMEM((2, 2 * n_tot, LANES), BF16), pltpu.VMEM((2, 2 * n_tot, LANES), F32),
               pltpu.VMEM((2, n_tot, LANES), BF16), f32buf(2), f32buf(2), f32buf(2)]
    return pl.pallas_call(
        functools.partial(_gdn_kernel, n_ctx=n_ctx, n_lat=n_lat),
        grid=(b, np_), in_specs=in_specs, out_specs=out_specs, out_shape=out_shape,
        scratch_shapes=scratch,
        compiler_params=_cparams(("arbitrary", "arbitrary")),
        name="gdn_mixer")(qkv_c, qkv_c, qkv_c, qkv_l, qkv_l, qkv_l,
                          a_c, a_c, a_c, a_l, a_l, a_l, z_c, z_l, gt_c, gt_l,
                          conv_qkv, conv_qkv, conv_qkv, conv_a, a8, dt8, gnorm2)


def _route(logits):
    lane = lax.broadcasted_iota(jnp.int32, logits.shape, 1)
    lane_f = lane.astype(F32)
    neg = -jnp.inf
    is_g = lane < N_GROUPS
    gl = jnp.where(is_g, logits, neg)
    gmax = jnp.max(gl, axis=-1, keepdims=True)
    grp = jnp.min(jnp.where(gl == gmax, lane_f, float(LANES)), axis=-1, keepdims=True)
    gsum = jnp.sum(jnp.where(is_g, jnp.exp(gl - gmax), 0.0), axis=-1, keepdims=True)
    gprob = 1.0 / gsum
    first = float(N_GROUPS) + grp * float(EXPERTS_PER_GROUP)
    ing = jnp.logical_and(lane_f >= first, lane_f < first + float(EXPERTS_PER_GROUP))
    el = jnp.where(ing, logits, neg)
    v1 = jnp.max(el, axis=-1, keepdims=True)
    i1 = jnp.min(jnp.where(el == v1, lane_f, float(LANES)), axis=-1, keepdims=True)
    el2 = jnp.where(lane_f == i1, neg, el)
    v2 = jnp.max(el2, axis=-1, keepdims=True)
    i2 = jnp.min(jnp.where(el2 == v2, lane_f, float(LANES)), axis=-1, keepdims=True)
    t = jnp.exp(v2 - v1)
    den = 1.0 + t
    w1 = (1.0 / den) * gprob
    w2 = (t / den) * gprob
    e1 = i1 - float(N_GROUPS)
    e2 = i2 - float(N_GROUPS)
    e_out = jnp.where(lane == 0, e1, jnp.where(lane == 1, e2, 0.0))
    w_out = jnp.where(lane == 0, w1, jnp.where(lane == 1, w2, 0.0))
    picked = jnp.where(jnp.logical_or(lane_f == e1, lane_f == e2), 1.0, 0.0)
    return e_out.astype(jnp.int32), w_out, jnp.sum(picked, axis=0, keepdims=True)


def _mix_out_kernel(*refs, n_parts):
    ys = refs[:n_parts]
    ws = refs[n_parts:2 * n_parts]
    x_ref, g1_ref, nw_ref, sc_ref, sh_ref, wr_ref = refs[2 * n_parts:2 * n_parts + 6]
    xo_ref, h2_ref, re_ref, rw_ref, cnt_ref = refs[2 * n_parts + 6:]
    y = jnp.dot(ys[0][0], ws[0][...], preferred_element_type=F32)
    for k in range(1, n_parts):
        y = y + jnp.dot(ys[k][0], ws[k][...], preferred_element_type=F32)
    x = x_ref[0] + g1_ref[0] * y
    xo_ref[0] = x
    h2 = _norm_mod(x, nw_ref[...], sc_ref[0], sh_ref[0])
    h2_ref[0] = h2
    h_hi = h2.astype(BF16)
    h_lo = (h2 - h_hi.astype(F32)).astype(BF16)
    both = jnp.dot(h_hi, wr_ref[...], preferred_element_type=F32)
    logits = (jnp.dot(h_lo, wr_ref[:, :LANES], preferred_element_type=F32) + both[:, LANES:]) + both[:, :LANES]
    e_out, w_out, cnt = _route(logits)
    re_ref[0] = e_out
    rw_ref[0] = w_out

    @pl.when(jnp.logical_and(pl.program_id(0) == 0, pl.program_id(1) == 0))
    def _():
        cnt_ref[...] = jnp.zeros_like(cnt_ref)

    cnt_ref[...] += cnt


def _mix_out(parts, weights, x, g1, nw, sc, sh, w_router, tm):
    b, l, d = x.shape
    n_parts = len(parts)
    per_batch = g1.shape[0] > 1
    mod_map = (lambda i, j: (i, 0, 0)) if per_batch else (lambda i, j: (0, 0, 0))
    row = lambda i, j: (i, j, 0)
    const2 = lambda i, j: (0, 0)
    in_specs = ([pl.BlockSpec((1, tm, p.shape[2]), row) for p in parts]
                + [pl.BlockSpec(w.shape, const2) for w in weights]
                + [pl.BlockSpec((1, tm, d), row), pl.BlockSpec((1, 1, d), mod_map),
                   pl.BlockSpec((1, d), const2), pl.BlockSpec((1, 1, d), mod_map),
                   pl.BlockSpec((1, 1, d), mod_map), pl.BlockSpec((d, 2 * LANES), const2)])
    return pl.pallas_call(
        functools.partial(_mix_out_kernel, n_parts=n_parts), grid=(b, l // tm),
        in_specs=in_specs,
        out_specs=[pl.BlockSpec((1, tm, d), row), pl.BlockSpec((1, tm, d), row),
                   pl.BlockSpec((1, tm, LANES), row), pl.BlockSpec((1, tm, LANES), row),
                   pl.BlockSpec((1, LANES), const2)],
        out_shape=[jax.ShapeDtypeStruct((b, l, d), F32), jax.ShapeDtypeStruct((b, l, d), F32),
                   jax.ShapeDtypeStruct((b, l, LANES), jnp.int32),
                   jax.ShapeDtypeStruct((b, l, LANES), F32),
                   jax.ShapeDtypeStruct((1, LANES), F32)],
        compiler_params=_cparams(("arbitrary", "arbitrary")),
        name="mix_out")(*parts, *weights, x, g1, nw, sc, sh, w_router)


def _block_layout(counts, n_blocks):
    counts = counts.astype(jnp.int32)
    padded = (counts + MOE_BLOCK - 1) // MOE_BLOCK * MOE_BLOCK
    pad_ends = jnp.cumsum(padded)
    pad_starts = pad_ends - padded
    blk_start = jnp.arange(n_blocks, dtype=jnp.int32) * MOE_BLOCK
    blk_expert = jnp.minimum(jnp.sum((blk_start[:, None] >= pad_ends[None, :]).astype(jnp.int32), axis=1),
                             N_EXPERTS - 1).astype(jnp.int32)
    blk_valid = (blk_start < pad_ends[-1]).astype(jnp.int32)
    return pad_starts, blk_expert, blk_valid


SLOT_TILE = 512


def _slot_kernel(re_ref, base_ref, slot_ref, carry):
    @pl.when(pl.program_id(0) == 0)
    def _():
        carry[...] = jnp.zeros_like(carry)

    e = re_ref[...]
    r = e.shape[0]
    lane = lax.broadcasted_iota(jnp.int32, e.shape, 1)
    oh0 = lane == e[:, 0:1]
    oh1 = lane == e[:, 1:2]
    both = jnp.where(jnp.logical_or(oh0, oh1), 1.0, 0.0)
    strict = (lax.broadcasted_iota(jnp.int32, (r, r), 0) > lax.broadcasted_iota(jnp.int32, (r, r), 1))
    before = jnp.dot(jnp.where(strict, 1.0, 0.0).astype(BF16), both.astype(BF16),
                     preferred_element_type=F32)
    pos = before + carry[...] + base_ref[...]
    s0 = jnp.sum(jnp.where(oh0, pos, 0.0), axis=-1, keepdims=True)
    s1 = jnp.sum(jnp.where(oh1, pos, 0.0), axis=-1, keepdims=True)
    slot_ref[...] = jnp.where(lane == 0, s0, jnp.where(lane == 1, s1, 0.0)).astype(jnp.int32)
    carry[...] += jnp.sum(both, axis=0, keepdims=True)


def _slots(re, base):
    t = re.shape[0]
    r = _row_tile(t, SLOT_TILE)
    return pl.pallas_call(
        _slot_kernel, grid=(t // r,),
        in_specs=[pl.BlockSpec((r, LANES), lambda i: (i, 0)), pl.BlockSpec((1, LANES), lambda i: (0, 0))],
        out_specs=pl.BlockSpec((r, LANES), lambda i: (i, 0)),
        out_shape=jax.ShapeDtypeStruct((t, LANES), jnp.int32),
        scratch_shapes=[pltpu.VMEM((1, LANES), F32)],
        compiler_params=_cparams(("arbitrary",)),
        name="slots")(re, base)


SCATTER_TILE = 256
SLAB = 8


def _scatter_kernel(idx_ref, h_ref, xs_in, xs_out, stage, idx_smem, ssem, isem, *, rt):
    del xs_in
    i = pl.program_id(0)
    n = pl.num_programs(0)
    slot = i % 2

    def drain(s):
        def body(r, carry):
            pltpu.make_async_copy(stage.at[s, pl.ds(0, SLAB)], xs_out.at[pl.ds(0, SLAB)], ssem.at[s]).wait()
            return carry
        lax.fori_loop(0, 2 * rt, body, 0, unroll=8)

    @pl.when(i > 1)
    def _():
        drain(slot)

    cp = pltpu.make_async_copy(idx_ref.at[0], idx_smem.at[pl.ds(slot, 1)], isem.at[0])
    cp.start()
    for s in range(SLAB):
        stage[slot, pl.ds(s, rt, stride=SLAB), :] = h_ref[:, s * LANES:(s + 1) * LANES]
    cp.wait()

    for r in range(rt):
        src = stage.at[slot, pl.ds(r * SLAB, SLAB)]
        d0 = pl.multiple_of(idx_smem[slot, r] * SLAB, SLAB)
        d1 = pl.multiple_of(idx_smem[slot, rt + r] * SLAB, SLAB)
        pltpu.make_async_copy(src, xs_out.at[pl.ds(d0, SLAB)], ssem.at[slot]).start(priority=0)
        pltpu.make_async_copy(src, xs_out.at[pl.ds(d1, SLAB)], ssem.at[slot]).start(priority=1)

    @pl.when(i == n - 1)
    def _():
        drain(slot)

        @pl.when(n > 1)
        def _():
            drain(1 - slot)


def _scatter_rows(h2, slot_of, xs):
    t, d = h2.shape
    assert d == SLAB * LANES
    rt = _row_tile(t, SCATTER_TILE)
    nt = t // rt
    idx = jnp.concatenate([slot_of[:, 0].reshape(nt, 1, rt), slot_of[:, 1].reshape(nt, 1, rt)], axis=2)
    return pl.pallas_call(
        functools.partial(_scatter_kernel, rt=rt), grid=(nt,),
        in_specs=[pl.BlockSpec((1, 1, 2 * rt), lambda i: (i, 0, 0)),
                  pl.BlockSpec((rt, d), lambda i: (i, 0)), pl.BlockSpec(memory_space=pl.ANY)],
        out_specs=pl.BlockSpec(memory_space=pl.ANY),
        out_shape=jax.ShapeDtypeStruct(xs.shape, xs.dtype),
        input_output_aliases={2: 0},
        scratch_shapes=[pltpu.VMEM((2, rt * SLAB, LANES), F32), pltpu.SMEM((2, 2 * rt), jnp.int32),
                        pltpu.SemaphoreType.DMA((2,)), pltpu.SemaphoreType.DMA((1,))],
        compiler_params=_cparams(("arbitrary",)),
        name="scatter_rows")(idx, h2, xs)


def _gather_rows(idx_vmem_ref, idx_smem, slot, src_hbm, dst_buf, isem, gsem, n_rows):
    cp = pltpu.make_async_copy(idx_vmem_ref.at[0], idx_smem.at[pl.ds(slot, 1)], isem.at[0])
    cp.start()
    cp.wait()

    for r in range(n_rows):
        tok = pl.multiple_of(idx_smem[slot, r] * SLAB, SLAB)
        pltpu.make_async_copy(src_hbm.at[pl.ds(tok, SLAB)], dst_buf.at[slot, pl.ds(r * SLAB, SLAB)],
                              gsem.at[slot]).start(priority=r % 2)


def _wait_rows(src_hbm, dst_buf, slot, gsem, n_rows):
    def body(r, carry):
        pltpu.make_async_copy(src_hbm.at[pl.ds(0, SLAB)], dst_buf.at[slot, pl.ds(0, SLAB)],
                              gsem.at[slot]).wait()
        return carry
    lax.fori_loop(0, n_rows, body, 0, unroll=8)


def _rows_from_slabs(ref, lead, first, n):
    return jnp.concatenate([ref[(*lead, pl.ds(first * SLAB + s, n, stride=SLAB), slice(None))]
                            for s in range(SLAB)], axis=1)


def _expert_kernel(be_ref, bv_ref, x_ref, wg_ref, wu_ref, wd_ref, out_ref):
    valid = bv_ref[pl.program_id(0)] > 0

    @pl.when(valid)
    def _():
        x = _rows_from_slabs(x_ref, (), 0, MOE_BLOCK).astype(BF16)
        g = jnp.dot(x, wg_ref[0].astype(BF16), preferred_element_type=F32)
        u = jnp.dot(x, wu_ref[0].astype(BF16), preferred_element_type=F32)
        h = (_silu(g) * u).astype(BF16)
        y = jnp.dot(h, wd_ref[0].astype(BF16), preferred_element_type=F32)
        for s in range(SLAB):
            out_ref[pl.ds(s, MOE_BLOCK, stride=SLAB), :] = y[:, s * LANES:(s + 1) * LANES]

    @pl.when(jnp.logical_not(valid))
    def _():
        out_ref[...] = jnp.zeros_like(out_ref)


def _experts(xs, blk_expert, blk_valid, w_gate, w_up, w_down):
    d = SLAB * LANES
    n_blocks = xs.shape[0] // (MOE_BLOCK * SLAB)
    blk = pl.BlockSpec((MOE_BLOCK * SLAB, LANES), lambda i, be, bv: (i, 0))
    grid_spec = pltpu.PrefetchScalarGridSpec(
        num_scalar_prefetch=2, grid=(n_blocks,),
        in_specs=[blk,
                  pl.BlockSpec((1, d, D_EXPERT), lambda i, be, bv: (be[i], 0, 0)),
                  pl.BlockSpec((1, d, D_EXPERT), lambda i, be, bv: (be[i], 0, 0)),
                  pl.BlockSpec((1, D_EXPERT, d), lambda i, be, bv: (be[i], 0, 0))],
        out_specs=blk)
    return pl.pallas_call(
        _expert_kernel, grid_spec=grid_spec,
        out_shape=jax.ShapeDtypeStruct(xs.shape, F32),
        compiler_params=_cparams(("arbitrary",)),
        name="experts")(blk_expert, blk_valid, xs, w_gate, w_up, w_down)


def _combine_kernel(idx_ref, idx_nxt_ref, y_hbm, x_ref, g2_ref, rw_ref, fn_ref, out_ref,
                    rbuf, idx_smem, gsem, isem, *, tm, final_norm):
    i = pl.program_id(0)
    n = pl.num_programs(0)
    slot = i % 2

    @pl.when(i == 0)
    def _():
        _gather_rows(idx_ref, idx_smem, 0, y_hbm, rbuf, isem, gsem, 2 * tm)

    @pl.when(i + 1 < n)
    def _():
        _gather_rows(idx_nxt_ref, idx_smem, 1 - slot, y_hbm, rbuf, isem, gsem, 2 * tm)

    _wait_rows(y_hbm, rbuf, slot, gsem, 2 * tm)
    rw = rw_ref[...]
    y = (rw[:, 0:1] * _rows_from_slabs(rbuf, (slot,), 0, tm)
         + rw[:, 1:2] * _rows_from_slabs(rbuf, (slot,), tm, tm))
    x = x_ref[...] + g2_ref[0] * y
    if final_norm:
        ms = jnp.mean(x * x, axis=-1, keepdims=True)
        x = (x * lax.rsqrt(ms + RMS_EPS)) * fn_ref[...]
    out_ref[...] = x


def _combine(ybuf, slot_of, x, g2, rw, fn, rows_per_batch, tm, final_norm):
    t, d = x.shape
    nt = t // tm
    idx = jnp.concatenate([slot_of[:, 0].reshape(nt, 1, tm), slot_of[:, 1].reshape(nt, 1, tm)], axis=2)
    per_batch = g2.shape[0] > 1
    tpb = rows_per_batch // tm
    mod_map = (lambda i: (i // tpb, 0, 0)) if per_batch else (lambda i: (0, 0, 0))
    return pl.pallas_call(
        functools.partial(_combine_kernel, tm=tm, final_norm=final_norm), grid=(nt,),
        in_specs=[pl.BlockSpec((1, 1, 2 * tm), lambda i: (i, 0, 0)),
                  pl.BlockSpec((1, 1, 2 * tm), lambda i: (jnp.minimum(i + 1, nt - 1), 0, 0)),
                  pl.BlockSpec(memory_space=pl.ANY),
                  pl.BlockSpec((tm, d), lambda i: (i, 0)),
                  pl.BlockSpec((1, 1, d), mod_map),
                  pl.BlockSpec((tm, LANES), lambda i: (i, 0)),
                  pl.BlockSpec((1, d), lambda i: (0, 0))],
        out_specs=pl.BlockSpec((tm, d), lambda i: (i, 0)),
        out_shape=jax.ShapeDtypeStruct((t, d), F32),
        scratch_shapes=[pltpu.VMEM((2, 2 * tm * SLAB, LANES), F32),
                        pltpu.SMEM((2, 2 * tm), jnp.int32),
                        pltpu.SemaphoreType.DMA((2,)),
                        pltpu.SemaphoreType.DMA((1,))],
        compiler_params=_cparams(("arbitrary",)),
        name="combine")(idx, idx, ybuf, x, g2, rw, fn)


def _proj1_kernel(x_ref, nw_ref, sc_ref, sh_ref, w_ref, qn_ref, kn_ref, cos_ref, sin_ref,
                  *out_refs, rope, want_q):
    h = _norm_mod(x_ref[0], nw_ref[...], sc_ref[0], sh_ref[0]).astype(BF16)
    hd = ATTN_HEAD_DIM
    nq = ATTN_HEADS * hd
    nkv = ATTN_KV_HEADS * hd

    def head_norm(xh, gain):
        ms = jnp.mean(xh * xh, axis=-1, keepdims=True)
        xh = (xh * lax.rsqrt(ms + RMS_EPS)) * gain
        if rope:
            xh = xh * cos_ref[...] + pltpu.roll(xh, hd // 2, 1) * sin_ref[...]
        return xh

    if want_q:
        q_ref, k_ref, v_ref = out_refs
        qf = jnp.dot(h, w_ref[:, 0:nq], preferred_element_type=F32)
        for i in range(ATTN_HEADS):
            xh = head_norm(qf[:, i * hd:(i + 1) * hd], qn_ref[...])
            q_ref[0, :, i * hd:(i + 1) * hd] = (xh * (hd ** -0.5)).astype(BF16)
    else:
        k_ref, v_ref = out_refs
    kf = jnp.dot(h, w_ref[:, nq:nq + nkv], preferred_element_type=F32)
    for i in range(ATTN_KV_HEADS):
        k_ref[0, :, i * hd:(i + 1) * hd] = head_norm(kf[:, i * hd:(i + 1) * hd], kn_ref[...]).astype(BF16)
    v_ref[0] = jnp.dot(h, w_ref[:, nq + nkv:], preferred_element_type=F32).astype(BF16)


def _proj1(x, nw, sc, sh, w, qn, kn, cos_t, sin_t, tm, rope, want_q):
    b, l, d = x.shape
    per_batch = sc.shape[0] > 1
    mod_map = (lambda i, j: (i, 0, 0)) if per_batch else (lambda i, j: (0, 0, 0))
    row = lambda i, j: (i, j, 0)
    const2 = lambda i, j: (0, 0)
    hd = ATTN_HEAD_DIM
    nq, nkv = ATTN_HEADS * hd, ATTN_KV_HEADS * hd
    out_specs = [pl.BlockSpec((1, tm, nkv), row), pl.BlockSpec((1, tm, nkv), row)]
    out_shape = [jax.ShapeDtypeStruct((b, l, nkv), BF16), jax.ShapeDtypeStruct((b, l, nkv), BF16)]
    if want_q:
        out_specs = [pl.BlockSpec((1, tm, nq), row)] + out_specs
        out_shape = [jax.ShapeDtypeStruct((b, l, nq), BF16)] + out_shape
    return pl.pallas_call(
        functools.partial(_proj1_kernel, rope=rope, want_q=want_q), grid=(b, l // tm),
        in_specs=[pl.BlockSpec((1, tm, d), row), pl.BlockSpec((1, d), const2),
                  pl.BlockSpec((1, 1, d), mod_map), pl.BlockSpec((1, 1, d), mod_map),
                  pl.BlockSpec(w.shape, const2), pl.BlockSpec((1, hd), const2),
                  pl.BlockSpec((1, hd), const2),
                  pl.BlockSpec((tm, hd), lambda i, j: (j, 0)), pl.BlockSpec((tm, hd), lambda i, j: (j, 0))],
        out_specs=out_specs, out_shape=out_shape,
        compiler_params=_cparams(("arbitrary", "arbitrary")),
        name="proj1")(x, nw, sc, sh, w, qn, kn, cos_t, sin_t)


def _attn_kernel(q_ref, kl_ref, vl_ref, kc_ref, vc_ref, o_ref):
    hd = ATTN_HEAD_DIM
    k_l, v_l, k_c, v_c = kl_ref[0], vl_ref[0], kc_ref[0], vc_ref[0]
    nt = (((1,), (1,)), ((), ()))
    for g in range(ATTN_GROUP):
        qh = q_ref[0, :, g * hd:(g + 1) * hd]
        s_l = lax.dot_general(qh, k_l, nt, preferred_element_type=F32)
        s_c = lax.dot_general(qh, k_c, nt, preferred_element_type=F32)
        m = jnp.maximum(jnp.max(s_l, axis=-1, keepdims=True), jnp.max(s_c, axis=-1, keepdims=True))
        p_l = jnp.exp(s_l - m)
        p_c = jnp.exp(s_c - m)
        den = jnp.sum(p_l, axis=-1, keepdims=True) + jnp.sum(p_c, axis=-1, keepdims=True)
        o = (jnp.dot(p_l.astype(BF16), v_l, preferred_element_type=F32)
             + jnp.dot(p_c.astype(BF16), v_c, preferred_element_type=F32))
        o_ref[0, :, g * hd:(g + 1) * hd] = (o / den).astype(BF16)


def _attention(q, k_l, v_l, k_c, v_c, tq):
    b, l, nq = q.shape
    n_ctx = k_c.shape[1]
    hd = ATTN_HEAD_DIM
    gw = ATTN_GROUP * hd
    return pl.pallas_call(
        _attn_kernel, grid=(b, ATTN_KV_HEADS, l // tq),
        in_specs=[pl.BlockSpec((1, tq, gw), lambda i, kv, j: (i, j, kv)),
                  pl.BlockSpec((1, l, hd), lambda i, kv, j: (i, 0, kv)),
                  pl.BlockSpec((1, l, hd), lambda i, kv, j: (i, 0, kv)),
                  pl.BlockSpec((1, n_ctx, hd), lambda i, kv, j: (i, 0, kv)),
                  pl.BlockSpec((1, n_ctx, hd), lambda i, kv, j: (i, 0, kv))],
        out_specs=pl.BlockSpec((1, tq, gw), lambda i, kv, j: (i, j, kv)),
        out_shape=jax.ShapeDtypeStruct((b, l, nq), BF16),
        compiler_params=_cparams(("arbitrary", "arbitrary", "arbitrary")),
        name="attention")(q, k_l, v_l, k_c, v_c)


def _row_tile(n, pref):
    return pref if n % pref == 0 else n


def _moe(streams, w_gate, w_up, w_down):
    d = streams[0][0].shape[1]
    t_all = sum(s[0].shape[0] for s in streams)
    n_blocks = -(-2 * t_all // MOE_BLOCK) + N_EXPERTS
    counts = sum(s[2] for s in streams)[0, :N_EXPERTS]
    pad_starts, blk_expert, blk_valid = _block_layout(counts, n_blocks)
    base = jnp.pad(pad_starts.astype(F32), (0, LANES - N_EXPERTS)).reshape(1, LANES)
    xs = jnp.zeros((n_blocks * MOE_BLOCK * SLAB, d // SLAB), F32)
    slot_tabs = []
    for h2, re, cnt in streams:
        slot_tab = _slots(re, base)
        xs = _scatter_rows(h2, slot_tab[:, :2], xs)
        slot_tabs.append(slot_tab)
        base = base + cnt
    ybuf = _experts(xs, blk_expert, blk_valid, w_gate, w_up, w_down)
    return ybuf, slot_tabs


def kernel(x, c, ctx, c_ctx, mod_w, mod_b, norm1, norm2, ab_w_in, ab_conv_a, ab_conv_qkv, ab_a_log,
           ab_dt_bias, ab_gnorm, ab_w_out, attn_w_qkv, attn_q_norm, attn_k_norm, attn_w_o, moe_w_group,
           moe_w_expert, moe_w_gate, moe_w_up, moe_w_down, final_norm):
    b, l, d = x.shape
    n_ctx = ctx.shape[1]
    assert mod_w.shape[0] == 2 and d == D_MODEL
    assert l % PRE_TILE == 0 and n_ctx % PRE_TILE == 0 and (l + n_ctx) % (SUPER * P1_UNITS) == 0
    tm_l = _row_tile(l, 512)
    tm_c = _row_tile(n_ctx, 256)

    rpad = (-(b + 1)) % 8
    cond = jnp.concatenate([c, c_ctx[None, :], jnp.zeros((rpad, d), F32)], axis=0)
    mod = _modulation(cond, mod_w, mod_b)

    def mods(i):
        lat = [mod[i, :b, k * d:(k + 1) * d].reshape(b, 1, d) for k in range(6)]
        cx = [mod[i, b:b + 1, k * d:(k + 1) * d].reshape(1, 1, d) for k in range(6)]
        return lat, cx

    def router_w(i):
        w = jnp.concatenate([moe_w_group[i], moe_w_expert[i]], axis=1)
        w = jnp.pad(w, ((0, 0), (0, LANES - w.shape[1])))
        w_hi = w.astype(BF16)
        w_lo = (w - w_hi.astype(F32)).astype(BF16)
        return jnp.concatenate([w_hi, w_lo], axis=1)

    (sh1, sc1, g1, sh2, sc2, g2), (csh1, csc1, cg1, csh2, csc2, cg2) = mods(0)
    w_in = ab_w_in[0]
    n_gate = 4 * GDN_HEADS
    w0 = jnp.concatenate([w_in[:, :P0_A + QKV_WIDTH + P0_Z], w_in[:, -n_gate:],
                          jnp.zeros((d, LANES - n_gate), F32)], axis=1).astype(BF16)
    nw1 = norm1[0].reshape(1, d)
    a_l, qkv_l, z_l, gt_l = _proj0(x, nw1, sc1, sh1, w0, tm_l)
    a_c, qkv_c, z_c, gt_c = _proj0(ctx, nw1, csc1, csh1, w0, tm_c)

    def pair_gates(gt):
        n = gt.shape[1]
        g4 = gt[:, :, :n_gate].reshape(b, n, 4, GDN_PAIRS, 2)
        return jnp.transpose(g4, (0, 3, 1, 2, 4)).reshape(b, GDN_PAIRS, n, 8)

    def pair_vec(v):
        v2 = v.reshape(2, GDN_PAIRS, 2)
        z2 = jnp.zeros_like(v2)
        return jnp.stack([v2[0], z2[0], v2[1], z2[1]], axis=1).reshape(GDN_PAIRS, 1, 8)

    gnorm2 = jnp.tile(ab_gnorm[0], 2).reshape(1, LANES)
    ya_c, yb_c, ya_l, yb_l = _gdn_mixer(
        a_c, qkv_c, z_c, pair_gates(gt_c), a_l, qkv_l, z_l, pair_gates(gt_l),
        ab_conv_qkv[0], ab_conv_a[0], pair_vec(ab_a_log[0]), pair_vec(ab_dt_bias[0]), gnorm2)

    w_out = ab_w_out[0].astype(BF16)
    w_out_a, w_out_b = w_out[:A_WIDTH], w_out[A_WIDTH:]
    nw2 = norm2[0].reshape(1, d)
    wr0 = router_w(0)
    x1, h2_l, re_l, rw_l, cnt_l = _mix_out([ya_l, yb_l], [w_out_a, w_out_b], x, g1, nw2, sc2, sh2, wr0, tm_l)
    c1, h2_c, re_c, rw_c, cnt_c = _mix_out([ya_c, yb_c], [w_out_a, w_out_b], ctx, cg1, nw2, csc2, csh2, wr0,
                                           tm_c)

    t_l, t_c = b * l, b * n_ctx
    ybuf, (slot_l, slot_c) = _moe([(h2_l.reshape(t_l, d), re_l.reshape(t_l, LANES), cnt_l),
                                   (h2_c.reshape(t_c, d), re_c.reshape(t_c, LANES), cnt_c)],
                                  moe_w_gate[0], moe_w_up[0], moe_w_down[0])
    fn = final_norm.reshape(1, d)
    x2 = _combine(ybuf, slot_l, x1.reshape(t_l, d), g2, rw_l.reshape(t_l, LANES), fn,
                  l, _row_tile(l, 256), False).reshape(b, l, d)
    ctx2 = _combine(ybuf, slot_c, c1.reshape(t_c, d), cg2, rw_c.reshape(t_c, LANES), fn,
                    n_ctx, _row_tile(n_ctx, 256), False).reshape(b, n_ctx, d)

    (sh1, sc1, g1, sh2, sc2, g2), (csh1, csc1, cg1, csh2, csc2, cg2) = mods(1)
    hd = ATTN_HEAD_DIM
    perm = jnp.concatenate([jnp.arange(0, hd, 2), jnp.arange(1, hd, 2)])
    n_qk = (ATTN_HEADS + ATTN_KV_HEADS) * hd
    col_perm = (jnp.arange(n_qk).reshape(-1, hd)[:, perm]).reshape(-1)
    w_qkv = attn_w_qkv[0]
    w1 = jnp.concatenate([w_qkv[:, col_perm], w_qkv[:, n_qk:]], axis=1).astype(BF16)
    qn = attn_q_norm[0][perm].reshape(1, hd)
    kn = attn_k_norm[0][perm].reshape(1, hd)
    pos = jnp.arange(l, dtype=jnp.int32)
    n_freq = hd // 4
    inv = ROPE_THETA ** (-jnp.arange(n_freq, dtype=F32) / n_freq)
    ang = jnp.concatenate([(pos // GRID_W).astype(F32)[:, None] * inv,
                           (pos % GRID_W).astype(F32)[:, None] * inv], axis=-1)
    cos_t = jnp.concatenate([jnp.cos(ang), jnp.cos(ang)], axis=-1)
    sin_t = jnp.concatenate([-jnp.sin(ang), jnp.sin(ang)], axis=-1)
    nw1 = norm1[1].reshape(1, d)
    q_l, k_l, v_l = _proj1(x2, nw1, sc1, sh1, w1, qn, kn, cos_t, sin_t, tm_l, True, True)
    dummy = jnp.zeros((n_ctx, hd), F32)
    k_c, v_c = _proj1(ctx2, nw1, csc1, csh1, w1, qn, kn, dummy, dummy, tm_c, False, False)
    o = _attention(q_l, k_l, v_l, k_c, v_c, _row_tile(l, 512))

    nw2 = norm2[1].reshape(1, d)
    x3, h2_l, re_l, rw_l, cnt_l = _mix_out([o], [attn_w_o[0].astype(BF16)], x2, g1, nw2, sc2, sh2,
                                           router_w(1), tm_l)
    ybuf, (slot_l,) = _moe([(h2_l.reshape(t_l, d), re_l.reshape(t_l, LANES), cnt_l)],
                           moe_w_gate[1], moe_w_up[1], moe_w_down[1])
    out = _combine(ybuf, slot_l, x3.reshape(t_l, d), g2, rw_l.reshape(t_l, LANES), fn,
                   l, _row_tile(l, 256), True)
    return out.reshape(b, l, d)
```

```python
import functools

import jax
import jax.numpy as jnp
from jax import lax
from jax.experimental import pallas as pl
from jax.experimental.pallas import tpu as pltpu

F32 = jnp.float32
BF16 = jnp.bfloat16
HI = lax.Precision.HIGHEST

D_MODEL = 1024
RMS_EPS = 1e-6
L2_EPS = 1e-6
GRID_W = 64
A_WIDTH = 512
GDN_HEADS = 8
GDN_DK = 64
GDN_DV = 64
GDN_CHUNK = 64
QKV_WIDTH = 1536
ATTN_HEADS = 8
ATTN_KV_HEADS = 2
ATTN_GROUP = 4
ATTN_HEAD_DIM = 128
ROPE_THETA = 10000.0
N_GROUPS = 8
EXPERTS_PER_GROUP = 8
N_EXPERTS = 64
D_EXPERT = 384
MOE_BLOCK = 256

LANES = 128
GDN_PAIRS = GDN_HEADS // 2
SUPER = 2 * GDN_CHUNK
VMEM_LIMIT = 52 * 1024 * 1024


def _cparams(sem):
    return pltpu.CompilerParams(dimension_semantics=sem, vmem_limit_bytes=VMEM_LIMIT)


def _silu(x):
    return x * jax.nn.sigmoid(x)


def _norm_mod(x, nw, sc, sh):
    ms = jnp.mean(x * x, axis=-1, keepdims=True)
    return (x * lax.rsqrt(ms + RMS_EPS)) * nw * (1.0 + sc) + sh


def _bdot(a, b):
    return jnp.dot(a.astype(BF16), b.astype(BF16), preferred_element_type=F32)


def _bdot_nt(a, b):
    return lax.dot_general(a.astype(BF16), b.astype(BF16), (((1,), (1,)), ((), ())),
                           preferred_element_type=F32)


def _bdot_tn(a, b):
    return lax.dot_general(a.astype(BF16), b.astype(BF16), (((0,), (0,)), ((), ())),
                           preferred_element_type=F32)


def _mod_kernel(c_ref, w_ref, b_ref, o_ref):
    c = c_ref[...]
    o_ref[0] = jnp.dot(_silu(c), w_ref[0], precision=HI, preferred_element_type=F32) + b_ref[0]


def _modulation(cond, mod_w, mod_b):
    depth, d, n = mod_w.shape
    r = cond.shape[0]
    tn = 512
    return pl.pallas_call(
        _mod_kernel, grid=(depth, n // tn),
        in_specs=[pl.BlockSpec((r, d), lambda i, j: (0, 0)),
                  pl.BlockSpec((1, d, tn), lambda i, j: (i, 0, j)),
                  pl.BlockSpec((1, 1, tn), lambda i, j: (i, 0, j))],
        out_specs=pl.BlockSpec((1, r, tn), lambda i, j: (i, 0, j)),
        out_shape=jax.ShapeDtypeStruct((depth, r, n), F32),
        compiler_params=_cparams(("arbitrary", "arbitrary")),
        name="modulation")(cond, mod_w, mod_b.reshape(depth, 1, n))


P0_A = 3 * A_WIDTH
P0_Z = GDN_HEADS * GDN_DV
P0_COLS = P0_A + QKV_WIDTH + P0_Z + LANES


def _proj0_kernel(x_ref, nw_ref, sc_ref, sh_ref, w_ref, a_ref, qkv_ref, z_ref, g_ref):
    h = _norm_mod(x_ref[0], nw_ref[...], sc_ref[0], sh_ref[0]).astype(BF16)
    c0, c1, c2 = P0_A, P0_A + QKV_WIDTH, P0_A + QKV_WIDTH + P0_Z
    a_ref[0] = jnp.dot(h, w_ref[:, 0:c0], preferred_element_type=F32).astype(BF16)
    qkv_ref[0] = jnp.dot(h, w_ref[:, c0:c1], preferred_element_type=F32).astype(BF16)
    z_ref[0] = jnp.dot(h, w_ref[:, c1:c2], preferred_element_type=F32).astype(BF16)
    g_ref[0] = jnp.dot(h, w_ref[:, c2:], preferred_element_type=F32)


def _proj0(x, nw, sc, sh, w, tm):
    b, l, d = x.shape
    per_batch = sc.shape[0] > 1
    mod_map = (lambda i, j: (i, 0, 0)) if per_batch else (lambda i, j: (0, 0, 0))
    row = lambda i, j: (i, j, 0)
    return pl.pallas_call(
        _proj0_kernel, grid=(b, l // tm),
        in_specs=[pl.BlockSpec((1, tm, d), row),
                  pl.BlockSpec((1, d), lambda i, j: (0, 0)),
                  pl.BlockSpec((1, 1, d), mod_map),
                  pl.BlockSpec((1, 1, d), mod_map),
                  pl.BlockSpec((d, P0_COLS), lambda i, j: (0, 0))],
        out_specs=[pl.BlockSpec((1, tm, P0_A), row),
                   pl.BlockSpec((1, tm, QKV_WIDTH), row),
                   pl.BlockSpec((1, tm, P0_Z), row),
                   pl.BlockSpec((1, tm, LANES), row)],
        out_shape=[jax.ShapeDtypeStruct((b, l, P0_A), BF16),
                   jax.ShapeDtypeStruct((b, l, QKV_WIDTH), BF16),
                   jax.ShapeDtypeStruct((b, l, P0_Z), BF16),
                   jax.ShapeDtypeStruct((b, l, LANES), F32)],
        compiler_params=_cparams(("arbitrary", "arbitrary")),
        name="proj0")(x, nw, sc, sh, w)


PRE_TILE = 256
P1_UNITS = 6


def _gdn_kernel(qc_ref, kc_ref, vc_ref, ql_ref, kl_ref, vl_ref,
                xac_ref, gbc_ref, gcc_ref, xal_ref, gbl_ref, gcl_ref,
                zc_ref, zl_ref, gtc_ref, gtl_ref,
                cwq_ref, cwk_ref, cwv_ref, cwa_ref, a8_ref, dt8_ref, gn_ref,
                yac_ref, ybc_ref, yal_ref, ybl_ref,
                qn_s, kn_s, vn_s, g_s, bt_s, m_s, qq_s, qe_s, ou_s, gl_s, o_s,
                *, n_ctx, n_lat):
    n_tot = n_ctx + n_lat
    lane = lax.broadcasted_iota(jnp.int32, (1, LANES), 1)
    lo = lane < GDN_DK
    lo2 = (lax.broadcasted_iota(jnp.int32, (1, 2 * LANES), 1) & (LANES - 1)) < GDN_DK
    head_blk = (lax.broadcasted_iota(jnp.int32, (LANES, LANES), 0) // GDN_DK
                == lax.broadcasted_iota(jnp.int32, (LANES, LANES), 1) // GDN_DK)
    blk_ones = jnp.where(head_blk, 1.0, 0.0).astype(BF16)

    def conv3_tile(ref, r0, n, w_ref, mul_ref=None):
        def rows(start, size):
            v = ref[0, pl.ds(start, size), :].astype(F32)
            if mul_ref is not None:
                v = v * mul_ref[0, pl.ds(start, size), :].astype(F32)
            return v
        cur = rows(r0, PRE_TILE)
        prev = rows(pl.multiple_of(jnp.maximum(r0 - 16, 0), 16), 16)[15:16, :] * (r0 > 0).astype(F32)
        nxt = (rows(pl.multiple_of(jnp.minimum(r0 + PRE_TILE, n - 16), 16), 16)[0:1, :]
               * (r0 + PRE_TILE < n).astype(F32))
        row = lax.broadcasted_iota(jnp.int32, (PRE_TILE, 1), 0)
        xm = jnp.where(row == 0, prev, pltpu.roll(cur, 1, 0))
        xp = jnp.where(row == PRE_TILE - 1, nxt, pltpu.roll(cur, PRE_TILE - 1, 0))
        return xm * w_ref[0:1, :] + cur * w_ref[1:2, :] + xp * w_ref[2:3, :]

    def head_sumsq(x):
        x2 = x * x
        hi = x2.astype(BF16)
        lo = (x2 - hi.astype(F32)).astype(BF16)
        return (jnp.dot(lo, blk_ones, preferred_element_type=F32)
                + jnp.dot(hi, blk_ones, preferred_element_type=F32))

    def prepass(seg_off, n, q_ref, k_ref, v_ref, gt_ref):
        def body(t, carry):
            r0 = pl.multiple_of(t * PRE_TILE, PRE_TILE)
            dst = pl.ds(pl.multiple_of(seg_off + r0, PRE_TILE), PRE_TILE)
            q = _silu(conv3_tile(q_ref, r0, n, cwq_ref))
            k = _silu(conv3_tile(k_ref, r0, n, cwk_ref))
            v = _silu(conv3_tile(v_ref, r0, n, cwv_ref))
            qn_s[dst, :] = q * lax.rsqrt(head_sumsq(q) + L2_EPS) * (GDN_DK ** -0.5)
            kn_s[dst, :] = k * lax.rsqrt(head_sumsq(k) + L2_EPS)
            vn_s[dst, :] = v
            gt = gt_ref[0, 0, pl.ds(r0, PRE_TILE), :]
            sp_in = gt + dt8_ref[0]
            softplus = jnp.maximum(sp_in, 0.0) + jnp.log1p(jnp.exp(-jnp.abs(sp_in)))
            graw = -jnp.exp(a8_ref[0]) * softplus
            beta = jax.nn.sigmoid(gt)
            for d in range(2):
                c = 4 * d
                g_s[d, dst, :] = jnp.where(lo, graw[:, c:c + 1], graw[:, c + 1:c + 2])
                bt_s[d, dst, :] = jnp.where(lo, beta[:, c + 2:c + 3], beta[:, c + 3:c + 4])
            return carry
        lax.fori_loop(0, n // PRE_TILE, body, 0)

    prepass(0, n_ctx, qc_ref, kc_ref, vc_ref, gtc_ref)
    prepass(n_ctx, n_lat, ql_ref, kl_ref, vl_ref, gtl_ref)

    ii = lax.broadcasted_iota(jnp.int32, (SUPER, SUPER), 0)
    jj = lax.broadcasted_iota(jnp.int32, (SUPER, SUPER), 1)
    xr = ii ^ jj
    same_chunk = xr < GDN_CHUNK
    top = lax.broadcasted_iota(jnp.int32, (SUPER, 1), 0) < GDN_CHUNK
    eye = jnp.where(ii == jj, 1.0, 0.0).astype(F32)
    cs_all = jnp.where(same_chunk, 1.0, 0.0).astype(F32)

    chains = ((0, 0), (0, 1), (1, 0), (1, 1))

    zero_b = jnp.zeros((SUPER, SUPER), BF16)

    def setup_unit(r0):
        rows = pl.ds(r0, SUPER)
        q_s = qn_s[rows, :]
        k_s = kn_s[rows, :]
        v_s = vn_s[rows, :]
        kb = k_s.astype(BF16)
        gram, qk = [], []
        for hh in range(2):
            hm = lo if hh == 0 else jnp.logical_not(lo)
            gram.append(_bdot_nt(jnp.where(hm, k_s, 0.0), kb))
            qk.append(_bdot_nt(jnp.where(hm, q_s, 0.0), kb))
        valid = [jnp.logical_and(same_chunk, jj <= ii), jnp.logical_and(same_chunk, jj >= ii)]
        b_l = [bt_s[0, rows, :], bt_s[1, rows, :]]
        sel = jnp.concatenate([jnp.where(valid[0], 1.0, 0.0), jnp.where(valid[1], 1.0, 0.0), cs_all],
                              axis=0).astype(BF16)
        g_rem = jnp.concatenate([g_s[0, rows, :], g_s[1, rows, :]], axis=1)
        parts = []
        for _ in range(3):
            part = g_rem.astype(BF16)
            parts.append(jnp.dot(sel, part, preferred_element_type=F32))
            g_rem = g_rem - part.astype(F32)
        cums = (parts[2] + parts[1]) + parts[0]
        gc = [cums[0:SUPER, 0:LANES], cums[SUPER:2 * SUPER, LANES:]]
        gcl = [cums[2 * SUPER:, 0:LANES], cums[2 * SUPER:, LANES:]]
        u = dict(r0=r0, rows=rows, qg=[], kg=[], rhs=[], nm=[], att=[], t_inv=[])
        gct = []
        for d in range(2):
            egc = jnp.exp(gc[d])
            u['qg'].append(q_s * egc)
            u['kg'].append((k_s * jnp.exp(gcl[d] - gc[d])).astype(BF16))
            gl_s[d, rows, :] = jnp.exp(gcl[d])
            u['rhs'].append(jnp.concatenate([v_s * b_l[d], k_s * b_l[d] * egc], axis=1).astype(BF16))
            gct.append(gc[d].T)
        for d, hh in chains:
            c0 = hh * GDN_DK
            diff = gc[d][:, c0:c0 + 1] - gct[d][c0:c0 + 1, :]
            dec = jnp.exp(jnp.where(valid[d], diff, -jnp.inf))
            u['nm'].append(jnp.where(ii == jj, 0.0, b_l[d][:, c0:c0 + 1] * gram[hh] * dec).astype(BF16))
            u['att'].append((qk[hh] * dec).astype(BF16))
            u['t_inv'].append(eye - jnp.where(xr < 2, u['nm'][-1].astype(F32), 0.0))
        return u

    def finish_unit(u, uw, auw):
        for d in range(2):
            c_a, c_b = 2 * d, 2 * d + 1
            uw_d = jnp.where(lo2, uw[c_a], uw[c_b])
            auw_d = jnp.where(lo2, auw[c_a], auw[c_b])
            ou_s[d, u['rows'], :] = auw_d[:, :LANES]
            qe_s[d, u['rows'], :] = (u['qg'][d] - auw_d[:, LANES:]).astype(BF16)
            uw_b = uw_d.astype(BF16)
            kg = u['kg'][d]
            for half in range(2):
                keep = top if half == 0 else jnp.logical_not(top)
                mq = _bdot_tn(jnp.where(keep, kg, jnp.zeros_like(kg)), uw_b)
                crow = pl.ds(pl.multiple_of(2 * u['r0'] + half * LANES, LANES), LANES)
                qq_s[d, crow, :] = jnp.where(head_blk, mq[:, :LANES], 0.0)
                m_s[d, crow, :] = jnp.where(head_blk, mq[:, LANES:], 0.0).astype(BF16)

    def phase1(s, carry):
        units = [setup_unit(pl.multiple_of((s * P1_UNITS + k) * SUPER, SUPER)) for k in range(P1_UNITS)]
        nm = [m for u in units for m in u['nm']]
        t_inv = [t for u in units for t in u['t_inv']]
        nc = len(nm)
        for lvl in range(1, 6):
            sel = (xr >> lvl) == 1
            tb = [t.astype(BF16) for t in t_inv]
            ot = [jnp.dot(jnp.where(sel, nm[c], zero_b), tb[c], preferred_element_type=F32) for c in range(nc)]
            tot = [jnp.dot(tb[c], ot[c].astype(BF16), preferred_element_type=F32) for c in range(nc)]
            t_inv = [t_inv[c] - tot[c] for c in range(nc)]
        rhs = [u['rhs'][d] for u in units for d, _ in chains]
        att = [a for u in units for a in u['att']]
        uw = [jnp.dot(t_inv[c].astype(BF16), rhs[c], preferred_element_type=F32) for c in range(nc)]
        auw = [jnp.dot(att[c], uw[c].astype(BF16), preferred_element_type=F32) for c in range(nc)]
        for k, u in enumerate(units):
            finish_unit(u, uw[4 * k:4 * k + 4], auw[4 * k:4 * k + 4])
        return carry

    lax.fori_loop(0, n_tot // (SUPER * P1_UNITS), phase1, 0)

    n_ch = n_tot // GDN_CHUNK
    n_cc = n_ctx // GDN_CHUNK

    def scan_step(n, states):
        c_f = n
        c_b = jnp.where(n < n_cc, n_cc - 1 - n, n_ch - 1 + n_cc - n)
        new_states = []
        for d, c in ((0, c_f), (1, c_b)):
            st = states[d]
            rows = pl.ds(pl.multiple_of(c * GDN_CHUNK, GDN_CHUNK), GDN_CHUNK)
            crow = pl.ds(pl.multiple_of(c * LANES, LANES), LANES)
            lhs = jnp.concatenate([m_s[d, crow, :], qe_s[d, rows, :]], axis=0)
            res = jnp.dot(lhs, st.astype(BF16), preferred_element_type=F32)
            o_s[d, rows, :] = res[LANES:] + ou_s[d, rows, :]
            gl = gl_s[d, pl.ds(pl.multiple_of(c * GDN_CHUNK, GDN_CHUNK), 1), :]
            new_states.append(st * gl + qq_s[d, crow, :] - res[:LANES])
        return tuple(new_states)

    s0 = jnp.zeros((LANES, LANES), F32)
    lax.fori_loop(0, n_ch, scan_step, (s0, s0))

    def postpass(seg_off, n, z_ref, xa_ref, gb_ref, gcv_ref, ya_ref, yb_ref):
        def body(t, carry):
            r0 = pl.multiple_of(t * PRE_TILE, PRE_TILE)
            src = pl.ds(pl.multiple_of(seg_off + r0, PRE_TILE), PRE_TILE)
            dst = pl.ds(r0, PRE_TILE)
            o = o_s[0, src, :] + o_s[1, src, :]
            ms = head_sumsq(o) * (1.0 / GDN_DV)
            z = z_ref[0, dst, :].astype(F32)
            yb_ref[0, dst, :] = ((o * lax.rsqrt(ms + RMS_EPS)) * gn_ref[...] * _silu(z)).astype(BF16)
            conv = conv3_tile(gcv_ref, r0, n, cwa_ref, mul_ref=xa_ref)
            ya_ref[0, dst, :] = (gb_ref[0, dst, :].astype(F32) * conv).astype(BF16)
            return carry
        lax.fori_loop(0, n // PRE_TILE, body, 0)

    postpass(0, n_ctx, zc_ref, xac_ref, gbc_ref, gcc_ref, yac_ref, ybc_ref)
    postpass(n_ctx, n_lat, zl_ref, xal_ref, gbl_ref, gcl_ref, yal_ref, ybl_ref)


def _gdn_mixer(a_c, qkv_c, z_c, gt_c, a_l, qkv_l, z_l, gt_l, conv_qkv, conv_a, a8, dt8, gnorm2):
    b, n_ctx, _ = qkv_c.shape
    n_lat = qkv_l.shape[1]
    n_tot = n_ctx + n_lat
    np_ = GDN_PAIRS

    def seq(n, off):
        return pl.BlockSpec((1, n, LANES), lambda i, p, off=off: (i, 0, off + p))

    def cw(off):
        return pl.BlockSpec((3, LANES), lambda i, p, off=off: (0, off + p))

    gate = lambda n: pl.BlockSpec((1, 1, n, 8), lambda i, p: (i, p, 0, 0))
    vec8 = pl.BlockSpec((1, 1, 8), lambda i, p: (p, 0, 0))
    in_specs = [seq(n_ctx, 0), seq(n_ctx, np_), seq(n_ctx, 2 * np_),
                seq(n_lat, 0), seq(n_lat, np_), seq(n_lat, 2 * np_),
                seq(n_ctx, 0), seq(n_ctx, np_), seq(n_ctx, 2 * np_),
                seq(n_lat, 0), seq(n_lat, np_), seq(n_lat, 2 * np_),
                seq(n_ctx, 0), seq(n_lat, 0), gate(n_ctx), gate(n_lat),
                cw(0), cw(np_), cw(2 * np_), cw(0), vec8, vec8,
                pl.BlockSpec((1, LANES), lambda i, p: (0, 0))]
    out_specs = [seq(n_ctx, 0), seq(n_ctx, 0), seq(n_lat, 0), seq(n_lat, 0)]
    out_shape = [jax.ShapeDtypeStruct((b, n_ctx, A_WIDTH), BF16),
                 jax.ShapeDtypeStruct((b, n_ctx, GDN_HEADS * GDN_DV), BF16),
                 jax.ShapeDtypeStruct((b, n_lat, A_WIDTH), BF16),
                 jax.ShapeDtypeStruct((b, n_lat, GDN_HEADS * GDN_DV), BF16)]
    f32buf = lambda *lead: pltpu.VMEM((*lead, n_tot, LANES), F32)
    scratch = [f32buf(), f32buf(), f32buf(), f32buf(2), f32buf(2),
               pltpu.VMEM((2, 2 * n_tot, LANES), BF16), pltpu.VMEM((2, 2 * n_tot, LANES), F32),
               pltpu.VMEM((2, n_tot, LANES), BF16), f32buf(2), f32buf(2), f32buf(2)]
    return pl.pallas_call(
        functools.partial(_gdn_kernel, n_ctx=n_ctx, n_lat=n_lat),
        grid=(b, np_), in_specs=in_specs, out_specs=out_specs, out_shape=out_shape,
        scratch_shapes=scratch,
        compiler_params=_cparams(("arbitrary", "arbitrary")),
        name="gdn_mixer")(qkv_c, qkv_c, qkv_c, qkv_l, qkv_l, qkv_l,
                          a_c, a_c, a_c, a_l, a_l, a_l, z_c, z_l, gt_c, gt_l,
                          conv_qkv, conv_qkv, conv_qkv, conv_a, a8, dt8, gnorm2)


def _route(logits):
    lane = lax.broadcasted_iota(jnp.int32, logits.shape, 1)
    lane_f = lane.astype(F32)
    neg = -jnp.inf
    is_g = lane < N_GROUPS
    gl = jnp.where(is_g, logits, neg)
    gmax = jnp.max(gl, axis=-1, keepdims=True)
    grp = jnp.min(jnp.where(gl == gmax, lane_f, float(LANES)), axis=-1, keepdims=True)
    gsum = jnp.sum(jnp.where(is_g, jnp.exp(gl - gmax), 0.0), axis=-1, keepdims=True)
    gprob = 1.0 / gsum
    first = float(N_GROUPS) + grp * float(EXPERTS_PER_GROUP)
    ing = jnp.logical_and(lane_f >= first, lane_f < first + float(EXPERTS_PER_GROUP))
    el = jnp.where(ing, logits, neg)
    v1 = jnp.max(el, axis=-1, keepdims=True)
    i1 = jnp.min(jnp.where(el == v1, lane_f, float(LANES)), axis=-1, keepdims=True)
    el2 = jnp.where(lane_f == i1, neg, el)
    v2 = jnp.max(el2, axis=-1, keepdims=True)
    i2 = jnp.min(jnp.where(el2 == v2, lane_f, float(LANES)), axis=-1, keepdims=True)
    t = jnp.exp(v2 - v1)
    den = 1.0 + t
    w1 = (1.0 / den) * gprob
    w2 = (t / den) * gprob
    e1 = i1 - float(N_GROUPS)
    e2 = i2 - float(N_GROUPS)
    e_out = jnp.where(lane == 0, e1, jnp.where(lane == 1, e2, 0.0))
    w_out = jnp.where(lane == 0, w1, jnp.where(lane == 1, w2, 0.0))
    picked = jnp.where(jnp.logical_or(lane_f == e1, lane_f == e2), 1.0, 0.0)
    return e_out.astype(jnp.int32), w_out, jnp.sum(picked, axis=0, keepdims=True)


def _mix_out_kernel(*refs, n_parts):
    ys = refs[:n_parts]
    ws = refs[n_parts:2 * n_parts]
    x_ref, g1_ref, nw_ref, sc_ref, sh_ref, wr_ref = refs[2 * n_parts:2 * n_parts + 6]
    xo_ref, h2_ref, re_ref, rw_ref, cnt_ref = refs[2 * n_parts + 6:]
    y = jnp.dot(ys[0][0], ws[0][...], preferred_element_type=F32)
    for k in range(1, n_parts):
        y = y + jnp.dot(ys[k][0], ws[k][...], preferred_element_type=F32)
    x = x_ref[0] + g1_ref[0] * y
    xo_ref[0] = x
    h2 = _norm_mod(x, nw_ref[...], sc_ref[0], sh_ref[0])
    h2_ref[0] = h2
    h_hi = h2.astype(BF16)
    h_lo = (h2 - h_hi.astype(F32)).astype(BF16)
    both = jnp.dot(h_hi, wr_ref[...], preferred_element_type=F32)
    logits = (jnp.dot(h_lo, wr_ref[:, :LANES], preferred_element_type=F32) + both[:, LANES:]) + both[:, :LANES]
    e_out, w_out, cnt = _route(logits)
    re_ref[0] = e_out
    rw_ref[0] = w_out

    @pl.when(jnp.logical_and(pl.program_id(0) == 0, pl.program_id(1) == 0))
    def _():
        cnt_ref[...] = jnp.zeros_like(cnt_ref)

    cnt_ref[...] += cnt


def _mix_out(parts, weights, x, g1, nw, sc, sh, w_router, tm):
    b, l, d = x.shape
    n_parts = len(parts)
    per_batch = g1.shape[0] > 1
    mod_map = (lambda i, j: (i, 0, 0)) if per_batch else (lambda i, j: (0, 0, 0))
    row = lambda i, j: (i, j, 0)
    const2 = lambda i, j: (0, 0)
    in_specs = ([pl.BlockSpec((1, tm, p.shape[2]), row) for p in parts]
                + [pl.BlockSpec(w.shape, const2) for w in weights]
                + [pl.BlockSpec((1, tm, d), row), pl.BlockSpec((1, 1, d), mod_map),
                   pl.BlockSpec((1, d), const2), pl.BlockSpec((1, 1, d), mod_map),
                   pl.BlockSpec((1, 1, d), mod_map), pl.BlockSpec((d, 2 * LANES), const2)])
    return pl.pallas_call(
        functools.partial(_mix_out_kernel, n_parts=n_parts), grid=(b, l // tm),
        in_specs=in_specs,
        out_specs=[pl.BlockSpec((1, tm, d), row), pl.BlockSpec((1, tm, d), row),
                   pl.BlockSpec((1, tm, LANES), row), pl.BlockSpec((1, tm, LANES), row),
                   pl.BlockSpec((1, LANES), const2)],
        out_shape=[jax.ShapeDtypeStruct((b, l, d), F32), jax.ShapeDtypeStruct((b, l, d), F32),
                   jax.ShapeDtypeStruct((b, l, LANES), jnp.int32),
                   jax.ShapeDtypeStruct((b, l, LANES), F32),
                   jax.ShapeDtypeStruct((1, LANES), F32)],
        compiler_params=_cparams(("arbitrary", "arbitrary")),
        name="mix_out")(*parts, *weights, x, g1, nw, sc, sh, w_router)


def _block_layout(counts, n_blocks):
    counts = counts.astype(jnp.int32)
    padded = (counts + MOE_BLOCK - 1) // MOE_BLOCK * MOE_BLOCK
    pad_ends = jnp.cumsum(padded)
    pad_starts = pad_ends - padded
    blk_start = jnp.arange(n_blocks, dtype=jnp.int32) * MOE_BLOCK
    blk_expert = jnp.minimum(jnp.sum((blk_start[:, None] >= pad_ends[None, :]).astype(jnp.int32), axis=1),
                             N_EXPERTS - 1).astype(jnp.int32)
    blk_valid = (blk_start < pad_ends[-1]).astype(jnp.int32)
    return pad_starts, blk_expert, blk_valid


SLOT_TILE = 512


def _slot_kernel(re_ref, base_ref, slot_ref, carry):
    @pl.when(pl.program_id(0) == 0)
    def _():
        carry[...] = jnp.zeros_like(carry)

    e = re_ref[...]
    r = e.shape[0]
    lane = lax.broadcasted_iota(jnp.int32, e.shape, 1)
    oh0 = lane == e[:, 0:1]
    oh1 = lane == e[:, 1:2]
    both = jnp.where(jnp.logical_or(oh0, oh1), 1.0, 0.0)
    strict = (lax.broadcasted_iota(jnp.int32, (r, r), 0) > lax.broadcasted_iota(jnp.int32, (r, r), 1))
    before = jnp.dot(jnp.where(strict, 1.0, 0.0).astype(BF16), both.astype(BF16),
                     preferred_element_type=F32)
    pos = before + carry[...] + base_ref[...]
    s0 = jnp.sum(jnp.where(oh0, pos, 0.0), axis=-1, keepdims=True)
    s1 = jnp.sum(jnp.where(oh1, pos, 0.0), axis=-1, keepdims=True)
    slot_ref[...] = jnp.where(lane == 0, s0, jnp.where(lane == 1, s1, 0.0)).astype(jnp.int32)
    carry[...] += jnp.sum(both, axis=0, keepdims=True)


def _slots(re, base):
    t = re.shape[0]
    r = _row_tile(t, SLOT_TILE)
    return pl.pallas_call(
        _slot_kernel, grid=(t // r,),
        in_specs=[pl.BlockSpec((r, LANES), lambda i: (i, 0)), pl.BlockSpec((1, LANES), lambda i: (0, 0))],
        out_specs=pl.BlockSpec((r, LANES), lambda i: (i, 0)),
        out_shape=jax.ShapeDtypeStruct((t, LANES), jnp.int32),
        scratch_shapes=[pltpu.VMEM((1, LANES), F32)],
        compiler_params=_cparams(("arbitrary",)),
        name="slots")(re, base)


SCATTER_TILE = 512
SLAB = 8


def _scatter_kernel(idx_ref, h_ref, xs_in, xs_out, stage, idx_smem, ssem, isem, *, rt):
    del xs_in
    i = pl.program_id(0)
    n = pl.num_programs(0)
    slot = i % 2

    def drain(s):
        def body(r, carry):
            pltpu.make_async_copy(stage.at[s, pl.ds(0, SLAB)], xs_out.at[pl.ds(0, SLAB)], ssem.at[s]).wait()
            return carry
        lax.fori_loop(0, 2 * rt, body, 0, unroll=8)

    @pl.when(i > 1)
    def _():
        drain(slot)

    cp = pltpu.make_async_copy(idx_ref.at[0], idx_smem.at[pl.ds(slot, 1)], isem.at[0])
    cp.start()
    for s in range(SLAB):
        stage[slot, pl.ds(s, rt, stride=SLAB), :] = h_ref[:, s * LANES:(s + 1) * LANES]
    cp.wait()

    for r in range(rt):
        src = stage.at[slot, pl.ds(r * SLAB, SLAB)]
        d0 = pl.multiple_of(idx_smem[slot, r] * SLAB, SLAB)
        d1 = pl.multiple_of(idx_smem[slot, rt + r] * SLAB, SLAB)
        pltpu.make_async_copy(src, xs_out.at[pl.ds(d0, SLAB)], ssem.at[slot]).start(priority=0)
        pltpu.make_async_copy(src, xs_out.at[pl.ds(d1, SLAB)], ssem.at[slot]).start(priority=1)

    @pl.when(i == n - 1)
    def _():
        drain(slot)

        @pl.when(n > 1)
        def _():
            drain(1 - slot)


def _scatter_rows(h2, slot_of, xs):
    t, d = h2.shape
    assert d == SLAB * LANES
    rt = _row_tile(t, SCATTER_TILE)
    nt = t // rt
    idx = jnp.concatenate([slot_of[:, 0].reshape(nt, 1, rt), slot_of[:, 1].reshape(nt, 1, rt)], axis=2)
    return pl.pallas_call(
        functools.partial(_scatter_kernel, rt=rt), grid=(nt,),
        in_specs=[pl.BlockSpec((1, 1, 2 * rt), lambda i: (i, 0, 0)),
                  pl.BlockSpec((rt, d), lambda i: (i, 0)), pl.BlockSpec(memory_space=pl.ANY)],
        out_specs=pl.BlockSpec(memory_space=pl.ANY),
        out_shape=jax.ShapeDtypeStruct(xs.shape, xs.dtype),
        input_output_aliases={2: 0},
        scratch_shapes=[pltpu.VMEM((2, rt * SLAB, LANES), F32), pltpu.SMEM((2, 2 * rt), jnp.int32),
                        pltpu.SemaphoreType.DMA((2,)), pltpu.SemaphoreType.DMA((1,))],
        compiler_params=_cparams(("arbitrary",)),
        name="scatter_rows")(idx, h2, xs)


def _gather_rows(idx_vmem_ref, idx_smem, slot, src_hbm, dst_buf, isem, gsem, n_rows):
    cp = pltpu.make_async_copy(idx_vmem_ref.at[0], idx_smem.at[pl.ds(slot, 1)], isem.at[0])
    cp.start()
    cp.wait()

    for r in range(n_rows):
        tok = pl.multiple_of(idx_smem[slot, r] * SLAB, SLAB)
        pltpu.make_async_copy(src_hbm.at[pl.ds(tok, SLAB)], dst_buf.at[slot, pl.ds(r * SLAB, SLAB)],
                              gsem.at[slot]).start(priority=r % 2)


def _wait_rows(src_hbm, dst_buf, slot, gsem, n_rows):
    def body(r, carry):
        pltpu.make_async_copy(src_hbm.at[pl.ds(0, SLAB)], dst_buf.at[slot, pl.ds(0, SLAB)],
                              gsem.at[slot]).wait()
        return carry
    lax.fori_loop(0, n_rows, body, 0, unroll=8)


def _rows_from_slabs(ref, lead, first, n):
    return jnp.concatenate([ref[(*lead, pl.ds(first * SLAB + s, n, stride=SLAB), slice(None))]
                            for s in range(SLAB)], axis=1)


def _expert_kernel(be_ref, bv_ref, x_ref, wg_ref, wu_ref, wd_ref, out_ref):
    valid = bv_ref[pl.program_id(0)] > 0

    @pl.when(valid)
    def _():
        x = _rows_from_slabs(x_ref, (), 0, MOE_BLOCK).astype(BF16)
        g = jnp.dot(x, wg_ref[0].astype(BF16), preferred_element_type=F32)
        u = jnp.dot(x, wu_ref[0].astype(BF16), preferred_element_type=F32)
        h = (_silu(g) * u).astype(BF16)
        y = jnp.dot(h, wd_ref[0].astype(BF16), preferred_element_type=F32)
        for s in range(SLAB):
            out_ref[pl.ds(s, MOE_BLOCK, stride=SLAB), :] = y[:, s * LANES:(s + 1) * LANES]

    @pl.when(jnp.logical_not(valid))
    def _():
        out_ref[...] = jnp.zeros_like(out_ref)


def _experts(xs, blk_expert, blk_valid, w_gate, w_up, w_down):
    d = SLAB * LANES
    n_blocks = xs.shape[0] // (MOE_BLOCK * SLAB)
    blk = pl.BlockSpec((MOE_BLOCK * SLAB, LANES), lambda i, be, bv: (i, 0))
    grid_spec = pltpu.PrefetchScalarGridSpec(
        num_scalar_prefetch=2, grid=(n_blocks,),
        in_specs=[blk,
                  pl.BlockSpec((1, d, D_EXPERT), lambda i, be, bv: (be[i], 0, 0)),
                  pl.BlockSpec((1, d, D_EXPERT), lambda i, be, bv: (be[i], 0, 0)),
                  pl.BlockSpec((1, D_EXPERT, d), lambda i, be, bv: (be[i], 0, 0))],
        out_specs=blk)
    return pl.pallas_call(
        _expert_kernel, grid_spec=grid_spec,
        out_shape=jax.ShapeDtypeStruct(xs.shape, F32),
        compiler_params=_cparams(("arbitrary",)),
        name="experts")(blk_expert, blk_valid, xs, w_gate, w_up, w_down)


def _combine_kernel(idx_ref, idx_nxt_ref, y_hbm, x_ref, g2_ref, rw_ref, fn_ref, out_ref,
                    rbuf, idx_smem, gsem, isem, *, tm, final_norm):
    i = pl.program_id(0)
    n = pl.num_programs(0)
    slot = i % 2

    @pl.when(i == 0)
    def _():
        _gather_rows(idx_ref, idx_smem, 0, y_hbm, rbuf, isem, gsem, 2 * tm)

    @pl.when(i + 1 < n)
    def _():
        _gather_rows(idx_nxt_ref, idx_smem, 1 - slot, y_hbm, rbuf, isem, gsem, 2 * tm)

    _wait_rows(y_hbm, rbuf, slot, gsem, 2 * tm)
    rw = rw_ref[...]
    y = (rw[:, 0:1] * _rows_from_slabs(rbuf, (slot,), 0, tm)
         + rw[:, 1:2] * _rows_from_slabs(rbuf, (slot,), tm, tm))
    x = x_ref[...] + g2_ref[0] * y
    if final_norm:
        ms = jnp.mean(x * x, axis=-1, keepdims=True)
        x = (x * lax.rsqrt(ms + RMS_EPS)) * fn_ref[...]
    out_ref[...] = x


def _combine(ybuf, slot_of, x, g2, rw, fn, rows_per_batch, tm, final_norm):
    t, d = x.shape
    nt = t // tm
    idx = jnp.concatenate([slot_of[:, 0].reshape(nt, 1, tm), slot_of[:, 1].reshape(nt, 1, tm)], axis=2)
    per_batch = g2.shape[0] > 1
    tpb = rows_per_batch // tm
    mod_map = (lambda i: (i // tpb, 0, 0)) if per_batch else (lambda i: (0, 0, 0))
    return pl.pallas_call(
        functools.partial(_combine_kernel, tm=tm, final_norm=final_norm), grid=(nt,),
        in_specs=[pl.BlockSpec((1, 1, 2 * tm), lambda i: (i, 0, 0)),
                  pl.BlockSpec((1, 1, 2 * tm), lambda i: (jnp.minimum(i + 1, nt - 1), 0, 0)),
                  pl.BlockSpec(memory_space=pl.ANY),
                  pl.BlockSpec((tm, d), lambda i: (i, 0)),
                  pl.BlockSpec((1, 1, d), mod_map),
                  pl.BlockSpec((tm, LANES), lambda i: (i, 0)),
                  pl.BlockSpec((1, d), lambda i: (0, 0))],
        out_specs=pl.BlockSpec((tm, d), lambda i: (i, 0)),
        out_shape=jax.ShapeDtypeStruct((t, d), F32),
        scratch_shapes=[pltpu.VMEM((2, 2 * tm * SLAB, LANES), F32),
                        pltpu.SMEM((2, 2 * tm), jnp.int32),
                        pltpu.SemaphoreType.DMA((2,)),
                        pltpu.SemaphoreType.DMA((1,))],
        compiler_params=_cparams(("arbitrary",)),
        name="combine")(idx, idx, ybuf, x, g2, rw, fn)


def _proj1_kernel(x_ref, nw_ref, sc_ref, sh_ref, w_ref, qn_ref, kn_ref, cos_ref, sin_ref,
                  *out_refs, rope, want_q):
    h = _norm_mod(x_ref[0], nw_ref[...], sc_ref[0], sh_ref[0]).astype(BF16)
    hd = ATTN_HEAD_DIM
    nq = ATTN_HEADS * hd
    nkv = ATTN_KV_HEADS * hd

    def head_norm(xh, gain):
        ms = jnp.mean(xh * xh, axis=-1, keepdims=True)
        xh = (xh * lax.rsqrt(ms + RMS_EPS)) * gain
        if rope:
            xh = xh * cos_ref[...] + pltpu.roll(xh, hd // 2, 1) * sin_ref[...]
        return xh

    if want_q:
        q_ref, k_ref, v_ref = out_refs
        qf = jnp.dot(h, w_ref[:, 0:nq], preferred_element_type=F32)
        for i in range(ATTN_HEADS):
            xh = head_norm(qf[:, i * hd:(i + 1) * hd], qn_ref[...])
            q_ref[0, :, i * hd:(i + 1) * hd] = (xh * (hd ** -0.5)).astype(BF16)
    else:
        k_ref, v_ref = out_refs
    kf = jnp.dot(h, w_ref[:, nq:nq + nkv], preferred_element_type=F32)
    for i in range(ATTN_KV_HEADS):
        k_ref[0, :, i * hd:(i + 1) * hd] = head_norm(kf[:, i * hd:(i + 1) * hd], kn_ref[...]).astype(BF16)
    v_ref[0] = jnp.dot(h, w_ref[:, nq + nkv:], preferred_element_type=F32).astype(BF16)


def _proj1(x, nw, sc, sh, w, qn, kn, cos_t, sin_t, tm, rope, want_q):
    b, l, d = x.shape
    per_batch = sc.shape[0] > 1
    mod_map = (lambda i, j: (i, 0, 0)) if per_batch else (lambda i, j: (0, 0, 0))
    row = lambda i, j: (i, j, 0)
    const2 = lambda i, j: (0, 0)
    hd = ATTN_HEAD_DIM
    nq, nkv = ATTN_HEADS * hd, ATTN_KV_HEADS * hd
    out_specs = [pl.BlockSpec((1, tm, nkv), row), pl.BlockSpec((1, tm, nkv), row)]
    out_shape = [jax.ShapeDtypeStruct((b, l, nkv), BF16), jax.ShapeDtypeStruct((b, l, nkv), BF16)]
    if want_q:
        out_specs = [pl.BlockSpec((1, tm, nq), row)] + out_specs
        out_shape = [jax.ShapeDtypeStruct((b, l, nq), BF16)] + out_shape
    return pl.pallas_call(
        functools.partial(_proj1_kernel, rope=rope, want_q=want_q), grid=(b, l // tm),
        in_specs=[pl.BlockSpec((1, tm, d), row), pl.BlockSpec((1, d), const2),
                  pl.BlockSpec((1, 1, d), mod_map), pl.BlockSpec((1, 1, d), mod_map),
                  pl.BlockSpec(w.shape, const2), pl.BlockSpec((1, hd), const2),
                  pl.BlockSpec((1, hd), const2),
                  pl.BlockSpec((tm, hd), lambda i, j: (j, 0)), pl.BlockSpec((tm, hd), lambda i, j: (j, 0))],
        out_specs=out_specs, out_shape=out_shape,
        compiler_params=_cparams(("arbitrary", "arbitrary")),
        name="proj1")(x, nw, sc, sh, w, qn, kn, cos_t, sin_t)


def _attn_kernel(q_ref, kl_ref, vl_ref, kc_ref, vc_ref, o_ref):
    hd = ATTN_HEAD_DIM
    k_l, v_l, k_c, v_c = kl_ref[0], vl_ref[0], kc_ref[0], vc_ref[0]
    nt = (((1,), (1,)), ((), ()))
    for g in range(ATTN_GROUP):
        qh = q_ref[0, :, g * hd:(g + 1) * hd]
        s_l = lax.dot_general(qh, k_l, nt, preferred_element_type=F32)
        s_c = lax.dot_general(qh, k_c, nt, preferred_element_type=F32)
        m = jnp.maximum(jnp.max(s_l, axis=-1, keepdims=True), jnp.max(s_c, axis=-1, keepdims=True))
        p_l = jnp.exp(s_l - m)
        p_c = jnp.exp(s_c - m)
        den = jnp.sum(p_l, axis=-1, keepdims=True) + jnp.sum(p_c, axis=-1, keepdims=True)
        o = (jnp.dot(p_l.astype(BF16), v_l, preferred_element_type=F32)
             + jnp.dot(p_c.astype(BF16), v_c, preferred_element_type=F32))
        o_ref[0, :, g * hd:(g + 1) * hd] = (o / den).astype(BF16)


def _attention(q, k_l, v_l, k_c, v_c, tq):
    b, l, nq = q.shape
    n_ctx = k_c.shape[1]
    hd = ATTN_HEAD_DIM
    gw = ATTN_GROUP * hd
    return pl.pallas_call(
        _attn_kernel, grid=(b, ATTN_KV_HEADS, l // tq),
        in_specs=[pl.BlockSpec((1, tq, gw), lambda i, kv, j: (i, j, kv)),
                  pl.BlockSpec((1, l, hd), lambda i, kv, j: (i, 0, kv)),
                  pl.BlockSpec((1, l, hd), lambda i, kv, j: (i, 0, kv)),
                  pl.BlockSpec((1, n_ctx, hd), lambda i, kv, j: (i, 0, kv)),
                  pl.BlockSpec((1, n_ctx, hd), lambda i, kv, j: (i, 0, kv))],
        out_specs=pl.BlockSpec((1, tq, gw), lambda i, kv, j: (i, j, kv)),
        out_shape=jax.ShapeDtypeStruct((b, l, nq), BF16),
        compiler_params=_cparams(("arbitrary", "arbitrary", "arbitrary")),
        name="attention")(q, k_l, v_l, k_c, v_c)


def _row_tile(n, pref):
    return pref if n % pref == 0 else n


def _moe(streams, w_gate, w_up, w_down):
    d = streams[0][0].shape[1]
    t_all = sum(s[0].shape[0] for s in streams)
    n_blocks = -(-2 * t_all // MOE_BLOCK) + N_EXPERTS
    counts = sum(s[2] for s in streams)[0, :N_EXPERTS]
    pad_starts, blk_expert, blk_valid = _block_layout(counts, n_blocks)
    base = jnp.pad(pad_starts.astype(F32), (0, LANES - N_EXPERTS)).reshape(1, LANES)
    xs = jnp.zeros((n_blocks * MOE_BLOCK * SLAB, d // SLAB), F32)
    slot_tabs = []
    for h2, re, cnt in streams:
        slot_tab = _slots(re, base)
        xs = _scatter_rows(h2, slot_tab[:, :2], xs)
        slot_tabs.append(slot_tab)
        base = base + cnt
    ybuf = _experts(xs, blk_expert, blk_valid, w_gate, w_up, w_down)
    return ybuf, slot_tabs


def kernel(x, c, ctx, c_ctx, mod_w, mod_b, norm1, norm2, ab_w_in, ab_conv_a, ab_conv_qkv, ab_a_log,
           ab_dt_bias, ab_gnorm, ab_w_out, attn_w_qkv, attn_q_norm, attn_k_norm, attn_w_o, moe_w_group,
           moe_w_expert, moe_w_gate, moe_w_up, moe_w_down, final_norm):
    b, l, d = x.shape
    n_ctx = ctx.shape[1]
    assert mod_w.shape[0] == 2 and d == D_MODEL
    assert l % PRE_TILE == 0 and n_ctx % PRE_TILE == 0 and (l + n_ctx) % (SUPER * P1_UNITS) == 0
    tm_l = _row_tile(l, 512)
    tm_c = _row_tile(n_ctx, 256)

    rpad = (-(b + 1)) % 8
    cond = jnp.concatenate([c, c_ctx[None, :], jnp.zeros((rpad, d), F32)], axis=0)
    mod = _modulation(cond, mod_w, mod_b)

    def mods(i):
        lat = [mod[i, :b, k * d:(k + 1) * d].reshape(b, 1, d) for k in range(6)]
        cx = [mod[i, b:b + 1, k * d:(k + 1) * d].reshape(1, 1, d) for k in range(6)]
        return lat, cx

    def router_w(i):
        w = jnp.concatenate([moe_w_group[i], moe_w_expert[i]], axis=1)
        w = jnp.pad(w, ((0, 0), (0, LANES - w.shape[1])))
        w_hi = w.astype(BF16)
        w_lo = (w - w_hi.astype(F32)).astype(BF16)
        return jnp.concatenate([w_hi, w_lo], axis=1)

    (sh1, sc1, g1, sh2, sc2, g2), (csh1, csc1, cg1, csh2, csc2, cg2) = mods(0)
    w_in = ab_w_in[0]
    n_gate = 4 * GDN_HEADS
    w0 = jnp.concatenate([w_in[:, :P0_A + QKV_WIDTH + P0_Z], w_in[:, -n_gate:],
                          jnp.zeros((d, LANES - n_gate), F32)], axis=1).astype(BF16)
    nw1 = norm1[0].reshape(1, d)
    a_l, qkv_l, z_l, gt_l = _proj0(x, nw1, sc1, sh1, w0, tm_l)
    a_c, qkv_c, z_c, gt_c = _proj0(ctx, nw1, csc1, csh1, w0, tm_c)

    def pair_gates(gt):
        n = gt.shape[1]
        g4 = gt[:, :, :n_gate].reshape(b, n, 4, GDN_PAIRS, 2)
        return jnp.transpose(g4, (0, 3, 1, 2, 4)).reshape(b, GDN_PAIRS, n, 8)

    def pair_vec(v):
        v2 = v.reshape(2, GDN_PAIRS, 2)
        z2 = jnp.zeros_like(v2)
        return jnp.stack([v2[0], z2[0], v2[1], z2[1]], axis=1).reshape(GDN_PAIRS, 1, 8)

    gnorm2 = jnp.tile(ab_gnorm[0], 2).reshape(1, LANES)
    ya_c, yb_c, ya_l, yb_l = _gdn_mixer(
        a_c, qkv_c, z_c, pair_gates(gt_c), a_l, qkv_l, z_l, pair_gates(gt_l),
        ab_conv_qkv[0], ab_conv_a[0], pair_vec(ab_a_log[0]), pair_vec(ab_dt_bias[0]), gnorm2)

    w_out = ab_w_out[0].astype(BF16)
    w_out_a, w_out_b = w_out[:A_WIDTH], w_out[A_WIDTH:]
    nw2 = norm2[0].reshape(1, d)
    wr0 = router_w(0)
    x1, h2_l, re_l, rw_l, cnt_l = _mix_out([ya_l, yb_l], [w_out_a, w_out_b], x, g1, nw2, sc2, sh2, wr0, tm_l)
    c1, h2_c, re_c, rw_c, cnt_c = _mix_out([ya_c, yb_c], [w_out_a, w_out_b], ctx, cg1, nw2, csc2, csh2, wr0,
                                           tm_c)

    t_l, t_c = b * l, b * n_ctx
    ybuf, (slot_l, slot_c) = _moe([(h2_l.reshape(t_l, d), re_l.reshape(t_l, LANES), cnt_l),
                                   (h2_c.reshape(t_c, d), re_c.reshape(t_c, LANES), cnt_c)],
                                  moe_w_gate[0], moe_w_up[0], moe_w_down[0])
    fn = final_norm.reshape(1, d)
    x2 = _combine(ybuf, slot_l, x1.reshape(t_l, d), g2, rw_l.reshape(t_l, LANES), fn,
                  l, _row_tile(l, 512), False).reshape(b, l, d)
    ctx2 = _combine(ybuf, slot_c, c1.reshape(t_c, d), cg2, rw_c.reshape(t_c, LANES), fn,
                    n_ctx, _row_tile(n_ctx, 256), False).reshape(b, n_ctx, d)

    (sh1, sc1, g1, sh2, sc2, g2), (csh1, csc1, cg1, csh2, csc2, cg2) = mods(1)
    hd = ATTN_HEAD_DIM
    perm = jnp.concatenate([jnp.arange(0, hd, 2), jnp.arange(1, hd, 2)])
    n_qk = (ATTN_HEADS + ATTN_KV_HEADS) * hd
    col_perm = (jnp.arange(n_qk).reshape(-1, hd)[:, perm]).reshape(-1)
    w_qkv = attn_w_qkv[0]
    w1 = jnp.concatenate([w_qkv[:, col_perm], w_qkv[:, n_qk:]], axis=1).astype(BF16)
    qn = attn_q_norm[0][perm].reshape(1, hd)
    kn = attn_k_norm[0][perm].reshape(1, hd)
    pos = jnp.arange(l, dtype=jnp.int32)
    n_freq = hd // 4
    inv = ROPE_THETA ** (-jnp.arange(n_freq, dtype=F32) / n_freq)
    ang = jnp.concatenate([(pos // GRID_W).astype(F32)[:, None] * inv,
                           (pos % GRID_W).astype(F32)[:, None] * inv], axis=-1)
    cos_t = jnp.concatenate([jnp.cos(ang), jnp.cos(ang)], axis=-1)
    sin_t = jnp.concatenate([-jnp.sin(ang), jnp.sin(ang)], axis=-1)
    nw1 = norm1[1].reshape(1, d)
    q_l, k_l, v_l = _proj1(x2, nw1, sc1, sh1, w1, qn, kn, cos_t, sin_t, tm_l, True, True)
    dummy = jnp.zeros((n_ctx, hd), F32)
    k_c, v_c = _proj1(ctx2, nw1, csc1, csh1, w1, qn, kn, dummy, dummy, tm_c, False, False)
    o = _attention(q_l, k_l, v_l, k_c, v_c, _row_tile(l, 1024))

    nw2 = norm2[1].reshape(1, d)
    x3, h2_l, re_l, rw_l, cnt_l = _mix_out([o], [attn_w_o[0].astype(BF16)], x2, g1, nw2, sc2, sh2,
                                           router_w(1), tm_l)
    ybuf, (slot_l,) = _moe([(h2_l.reshape(t_l, d), re_l.reshape(t_l, LANES), cnt_l)],
                           moe_w_gate[1], moe_w_up[1], moe_w_down[1])
    out = _combine(ybuf, slot_l, x3.reshape(t_l, d), g2, rw_l.reshape(t_l, LANES), fn,
                   l, _row_tile(l, 512), True)
    return out.reshape(b, l, d)
```
